```python
import math
import jax, jax.numpy as jnp
from jax import lax
import numpy as np

D_MODEL = 1024
BATCH = 16
SEQ = 2048
DEPTH = 2
DEC_BATCH = 16
DEC_SEQ = 4096
PAST_LEN = 128

GRID_W = 64
BLOCK = 128
N_BRANCH = 4
BRANCH_W = 512
CONV_K = 31
DIFF_HEADS = 4
DIFF_DH = 64
DIFF_VD = 2 * DIFF_DH
WIN = 128
WIN_HEADS = 8
WIN_KV = 2
WIN_DH = 64
AX_HEADS = 8
AX_KV = 2
AX_DH = 64
ROPE_THETA = 10000.0
MEM_LEN = 256
X_HEADS = 4
X_DH = D_MODEL // X_HEADS
FFN_DIM = 2752
FFN_CONV_K = 3
NUM_BUCKETS = 32
MAX_DISTANCE = 128
N_BIAS = 2 * DIFF_HEADS + WIN_HEADS
LN_EPS = 1e-5
NEG = -1e30
DN_ALPHA = (2 * DEPTH) ** 0.25
DN_BETA = (8 * DEPTH) ** -0.25
SPLITS = (2 * BRANCH_W,
          DIFF_HEADS * 2 * DIFF_DH, DIFF_HEADS * 2 * DIFF_DH, DIFF_HEADS * DIFF_VD,
          WIN_HEADS * WIN_DH, WIN_KV * WIN_DH, WIN_KV * WIN_DH,
          AX_HEADS * AX_DH, AX_KV * AX_DH, AX_KV * AX_DH,
          N_BRANCH * D_MODEL)
N_IN = sum(SPLITS)

kernel_name = 'hybrid_bidir_encoder_two_groups'


def layer_norm(x, g, b):
    xf = x.astype(jnp.float32)
    mu = jnp.mean(xf, -1, keepdims=True)
    var = jnp.mean(jnp.square(xf - mu), -1, keepdims=True)
    y = (xf - mu) * lax.rsqrt(var + LN_EPS) * g.astype(jnp.float32) + b.astype(jnp.float32)
    return y.astype(x.dtype)


def rms_norm(x, g):
    xf = x.astype(jnp.float32)
    y = xf * lax.rsqrt(jnp.mean(xf * xf, -1, keepdims=True) + LN_EPS) * g.astype(jnp.float32)
    return y.astype(x.dtype)


def depthwise_conv(x, w, b):
    k = w.shape[0]
    y = lax.conv_general_dilated(x, w[:, None, :].astype(x.dtype), window_strides=(1,),
                                 padding=[(k // 2, k // 2)],
                                 dimension_numbers=('NWC', 'WIO', 'NWC'),
                                 feature_group_count=x.shape[-1])
    return y + b


def t5_bucket(rel):
    half = NUM_BUCKETS // 2
    max_exact = half // 2
    n = jnp.abs(rel)
    nf = jnp.maximum(n, 1).astype(jnp.float32)
    large = max_exact + (jnp.log(nf / max_exact) / math.log(MAX_DISTANCE / max_exact)
                         * (half - max_exact)).astype(jnp.int32)
    large = jnp.minimum(large, half - 1)
    return jnp.where(rel > 0, half, 0) + jnp.where(n < max_exact, n, large)


def to_blocks(x):
    b, s = x.shape[:2]
    return jnp.moveaxis(x.reshape((b, s // BLOCK, BLOCK) + x.shape[2:]), 1, 0)


def from_blocks(y):
    nb, b = y.shape[:2]
    y = jnp.moveaxis(y, 0, 1)
    return y.reshape((b, nb * y.shape[2]) + y.shape[3:])


def conv_module(u, conv_w, conv_b, ln_g, ln_b):
    h = u[..., :BRANCH_W] * jax.nn.sigmoid(u[..., BRANCH_W:])
    h = depthwise_conv(h, conv_w, conv_b)
    h = layer_norm(h, ln_g, ln_b)
    return jax.nn.silu(h)


def diff_attention(q, k, v, lam, sub_g, bias_table, lam_init):
    b, s = q.shape[:2]
    lf = lam.astype(jnp.float32)
    lmb = jnp.exp(jnp.sum(lf[0] * lf[1])) - jnp.exp(jnp.sum(lf[2] * lf[3])) + lam_init
    scale = DIFF_DH ** -0.5
    kpos = jnp.arange(s)

    def block(args):
        i, qb = args
        qpos = i * BLOCK + jnp.arange(BLOCK)
        bias = bias_table[t5_bucket(kpos[None, :] - qpos[:, None])]
        bias = jnp.transpose(bias, (2, 3, 0, 1)).astype(jnp.float32)
        sc = jnp.einsum('bqhmd,bkhmd->bhmqk', qb, k).astype(jnp.float32) * scale + bias
        p = jax.nn.softmax(sc, axis=-1)
        a = p[:, :, 0] - lmb * p[:, :, 1]
        return jnp.einsum('bhqk,bkhe->bqhe', a.astype(v.dtype), v)

    o = from_blocks(lax.map(block, (jnp.arange(s // BLOCK), to_blocks(q))))
    o = rms_norm(o, sub_g) * (1.0 - lam_init)
    return o.reshape(b, s, DIFF_HEADS * DIFF_VD)


def window_attention(q, k, v, sink, bias_table):
    b, s = q.shape[:2]
    span = BLOCK + 2 * WIN
    scale = WIN_DH ** -0.5
    kp = jnp.pad(k, ((0, 0), (WIN, WIN), (0, 0), (0, 0)))
    vp = jnp.pad(v, ((0, 0), (WIN, WIN), (0, 0), (0, 0)))
    off = jnp.arange(span)[None, :] - WIN - jnp.arange(BLOCK)[:, None]
    bias = jnp.transpose(bias_table[t5_bucket(off)], (2, 3, 0, 1)).astype(jnp.float32)
    in_win = jnp.abs(off) <= WIN
    sink_f = sink.astype(jnp.float32)[None, :, :, None, None]

    def block(args):
        i, qb = args
        kb = lax.dynamic_slice_in_dim(kp, i * BLOCK, span, axis=1)
        vb = lax.dynamic_slice_in_dim(vp, i * BLOCK, span, axis=1)
        kpos = i * BLOCK - WIN + jnp.arange(span)
        mask = in_win & ((kpos >= 0) & (kpos < s))[None, :]
        sc = jnp.einsum('bqhgd,bkhd->bhgqk', qb, kb).astype(jnp.float32) * scale + bias
        sc = jnp.where(mask, sc, NEG)
        m = jnp.maximum(jnp.max(sc, -1, keepdims=True), sink_f)
        p = jnp.exp(sc - m)
        p = p / (jnp.sum(p, -1, keepdims=True) + jnp.exp(sink_f - m))
        return jnp.einsum('bhgqk,bkhd->bqhgd', p.astype(v.dtype), vb)

    o = from_blocks(lax.map(block, (jnp.arange(s // BLOCK), to_blocks(q))))
    return o.reshape(b, s, WIN_HEADS * WIN_DH)


def axial_angles(s):
    rows = s // GRID_W
    row = jnp.repeat(jnp.arange(rows, dtype=jnp.float32), GRID_W)
    col = jnp.tile(jnp.arange(GRID_W, dtype=jnp.float32), rows)
    n_freq = AX_DH // 4
    inv = ROPE_THETA ** (-jnp.arange(n_freq, dtype=jnp.float32) / n_freq)
    return row[:, None] * inv, col[:, None] * inv


def rot_half(x, ang):
    n = x.shape[-1] // 2
    c = jnp.cos(ang)[None, :, None, :]
    sn = jnp.sin(ang)[None, :, None, :]
    x1, x2 = x[..., :n], x[..., n:]
    return jnp.concatenate([x1 * c - x2 * sn, x2 * c + x1 * sn], -1)


def axial_rope(x, ang_r, ang_c):
    xf = x.astype(jnp.float32)
    h = AX_DH // 2
    return jnp.concatenate([rot_half(xf[..., :h], ang_r), rot_half(xf[..., h:], ang_c)], -1).astype(x.dtype)


def axial_attention(q, k, v, qn_g, kn_g):
    b, s = q.shape[:2]
    scale = AX_DH ** -0.5
    ang_r, ang_c = axial_angles(s)
    q = axial_rope(rms_norm(q, qn_g), ang_r, ang_c).reshape(b, s, AX_KV, AX_HEADS // AX_KV, AX_DH)
    k = axial_rope(rms_norm(k, kn_g), ang_r, ang_c)

    def block(qb):
        sc = jnp.einsum('bqhgd,bkhd->bhgqk', qb, k).astype(jnp.float32) * scale
        p = jax.nn.softmax(sc, axis=-1)
        return jnp.einsum('bhgqk,bkhd->bqhgd', p.astype(v.dtype), v)

    o = from_blocks(lax.map(block, to_blocks(q)))
    return o.reshape(b, s, AX_HEADS * AX_DH)


def token_mixing(x, layer, w_in, b_in, conv_w, conv_b, cln_g, cln_b, lam, sub_g, sink,
                 qn_g, kn_g, w_branch, w_out, diff_bias, win_bias):
    b, s, _ = x.shape
    u = x @ w_in + b_in
    offs = np.cumsum(SPLITS)[:-1].tolist()
    ua, bq, bk, bv, cq, ck, cv, dq, dk, dv, gl = jnp.split(u, offs, axis=-1)
    lam_init = 0.8 - 0.6 * math.exp(-0.3 * layer)
    o_a = conv_module(ua, conv_w, conv_b, cln_g, cln_b)
    o_b = diff_attention(bq.reshape(b, s, DIFF_HEADS, 2, DIFF_DH), bk.reshape(b, s, DIFF_HEADS, 2, DIFF_DH),
                         bv.reshape(b, s, DIFF_HEADS, DIFF_VD), lam, sub_g, diff_bias, lam_init)
    o_c = window_attention(cq.reshape(b, s, WIN_KV, WIN_HEADS // WIN_KV, WIN_DH),
                           ck.reshape(b, s, WIN_KV, WIN_DH), cv.reshape(b, s, WIN_KV, WIN_DH),
                           sink.reshape(WIN_KV, WIN_HEADS // WIN_KV), win_bias)
    o_d = axial_attention(dq.reshape(b, s, AX_HEADS, AX_DH), dk.reshape(b, s, AX_KV, AX_DH),
                          dv.reshape(b, s, AX_KV, AX_DH), qn_g, kn_g)
    gates = jax.nn.sigmoid(gl.reshape(b, s, N_BRANCH, D_MODEL))
    merged = gates[:, :, 0] * (o_a @ w_branch[0])
    for n, o in ((1, o_b), (2, o_c), (3, o_d)):
        merged = merged + gates[:, :, n] * (o @ w_branch[n])
    return merged @ w_out


def memory_cross_attention(x, mem, w_q, w_kv, w_o):
    b, s, _ = x.shape
    m = mem.shape[1]
    q = (x @ w_q).reshape(b, s, X_HEADS, X_DH)
    kv = (mem @ w_kv).reshape(b, m, 2, X_HEADS, X_DH)
    sc = jnp.einsum('bqhd,bkhd->bhqk', q, kv[:, :, 0]).astype(jnp.float32) * (X_DH ** -0.5)
    p = jax.nn.softmax(sc, axis=-1)
    o = jnp.einsum('bhqk,bkhd->bqhd', p.astype(x.dtype), kv[:, :, 1]).reshape(b, s, D_MODEL)
    return o @ w_o


def conv_ffn(x, w_up, conv_w, conv_b, w_down):
    h = depthwise_conv(x @ w_up, conv_w, conv_b)
    g, u = h[..., :FFN_DIM], h[..., FFN_DIM:]
    return (jax.nn.gelu(g, approximate=False) * u) @ w_down


def setup_inputs(seed: int = 0) -> dict:
    key = jax.random.key(seed)
    ks = jax.random.split(key, 32)
    f32 = jnp.float32
    L = DEPTH
    D = D_MODEL

    def nrm(i, shape, scale):
        return jax.random.normal(ks[i], shape, f32) * scale

    def gain(i, shape):
        return 1.0 + nrm(i, shape, 0.02)

    return {
        'x_prompt': nrm(0, (BATCH, SEQ, D), 1.0),
        'x_sample': nrm(1, (DEC_BATCH, DEC_SEQ, D), 1.0),
        'mem_prompt': nrm(2, (BATCH, MEM_LEN, D), 1.0),
        'mem_sample': nrm(3, (DEC_BATCH, MEM_LEN, D), 1.0),
        'rel_bias': nrm(4, (NUM_BUCKETS, N_BIAS), 0.1),
        'w_in': nrm(5, (L, D, N_IN), D ** -0.5),
        'b_in': nrm(6, (L, N_IN), 0.02),
        'a_conv_w': nrm(7, (L, CONV_K, BRANCH_W), CONV_K ** -0.5),
        'a_conv_b': nrm(8, (L, BRANCH_W), 0.02),
        'a_ln_g': gain(9, (L, BRANCH_W)),
        'a_ln_b': nrm(10, (L, BRANCH_W), 0.02),
        'diff_lam': nrm(11, (L, 4, DIFF_DH), 0.1),
        'diff_sub_g': gain(12, (L, DIFF_VD)),
        'win_sink': nrm(13, (L, WIN_HEADS), 0.5),
        'ax_qn_g': gain(14, (L, AX_DH)),
        'ax_kn_g': gain(15, (L, AX_DH)),
        'w_branch': nrm(16, (L, N_BRANCH, BRANCH_W, D), BRANCH_W ** -0.5),
        'w_mix_out': nrm(17, (L, D, D), D ** -0.5 * DN_BETA),
        'ln1_g': gain(18, (L, D)),
        'ln1_b': nrm(19, (L, D), 0.02),
        'w_xq': nrm(20, (L, D, D), D ** -0.5),
        'w_xkv': nrm(21, (L, D, 2 * D), D ** -0.5),
        'w_xo': nrm(22, (L, D, D), D ** -0.5 * DN_BETA),
        'ln2_g': gain(23, (L, D)),
        'ln2_b': nrm(24, (L, D), 0.02),
        'w_up': nrm(25, (L, D, 2 * FFN_DIM), D ** -0.5),
        'f_conv_w': nrm(26, (L, FFN_CONV_K, 2 * FFN_DIM), FFN_CONV_K ** -0.5),
        'f_conv_b': nrm(27, (L, 2 * FFN_DIM), 0.02),
        'w_down': nrm(28, (L, FFN_DIM, D), FFN_DIM ** -0.5 * DN_BETA),
        'ln3_g': gain(29, (L, D)),
        'ln3_b': nrm(30, (L, D), 0.02),
    }


def reference(x_prompt, x_sample, mem_prompt, mem_sample, rel_bias, w_in, b_in, a_conv_w, a_conv_b,
              a_ln_g, a_ln_b, diff_lam, diff_sub_g, win_sink, ax_qn_g, ax_kn_g, w_branch, w_mix_out,
              ln1_g, ln1_b, w_xq, w_xkv, w_xo, ln2_g, ln2_b, w_up, f_conv_w, f_conv_b, w_down,
              ln3_g, ln3_b):
    diff_bias = rel_bias[:, :2 * DIFF_HEADS].reshape(NUM_BUCKETS, DIFF_HEADS, 2)
    win_bias = rel_bias[:, 2 * DIFF_HEADS:].reshape(NUM_BUCKETS, WIN_KV, WIN_HEADS // WIN_KV)

    def encode(x, mem):
        for l in range(DEPTH):
            h = token_mixing(x, l, w_in[l], b_in[l], a_conv_w[l], a_conv_b[l], a_ln_g[l], a_ln_b[l],
                             diff_lam[l], diff_sub_g[l], win_sink[l], ax_qn_g[l], ax_kn_g[l],
                             w_branch[l], w_mix_out[l], diff_bias, win_bias)
            x = layer_norm(DN_ALPHA * x + h, ln1_g[l], ln1_b[l])
            h = memory_cross_attention(x, mem, w_xq[l], w_xkv[l], w_xo[l])
            x = layer_norm(DN_ALPHA * x + h, ln2_g[l], ln2_b[l])
            h = conv_ffn(x, w_up[l], f_conv_w[l], f_conv_b[l], w_down[l])
            x = layer_norm(DN_ALPHA * x + h, ln3_g[l], ln3_b[l])
        return x

    y_prompt = encode(x_prompt, mem_prompt)
    y_sample = encode(x_sample, mem_sample)
    return (y_prompt, y_sample)
```

```python
import functools
import math

import jax
import jax.numpy as jnp
from jax import lax
from jax.experimental import pallas as pl
from jax.experimental.pallas import tpu as pltpu

f32 = jnp.float32
bf16 = jnp.bfloat16

D_MODEL = 1024
DEPTH = 2
GRID_W = 64
N_BRANCH = 4
BRANCH_W = 512
CONV_K = 31
DIFF_HEADS = 4
DIFF_DH = 64
DIFF_VD = 2 * DIFF_DH
WIN = 128
WIN_HEADS = 8
WIN_KV = 2
WIN_DH = 64
AX_HEADS = 8
AX_KV = 2
AX_DH = 64
ROPE_THETA = 10000.0
X_HEADS = 4
X_DH = D_MODEL // X_HEADS
FFN_DIM = 2752
FFN_CONV_K = 3
NUM_BUCKETS = 32
MAX_DISTANCE = 128
LN_EPS = 1e-5
NEG = -1e30
DN_ALPHA = (2 * DEPTH) ** 0.25
N_IN = 8192

_COL_UA = 0
_COL_BQ = 8
_COL_BK = 12
_COL_BV = 16
_COL_CQ = 20
_COL_CK = 24
_COL_CV = 25
_COL_DQ = 26
_COL_DK = 30
_COL_DV = 31
_COL_GL = 32

LANE = 128
FFN_PAD = 2816
FFN_CHUNK = FFN_PAD // 2
VMEM_LIMIT = 48 * 1024 * 1024

_NT = (((1,), (1,)), ((), ()))


def _cparams(n_axes):
    return pltpu.CompilerParams(dimension_semantics=("arbitrary",) * n_axes,
                                vmem_limit_bytes=VMEM_LIMIT)


def _layer_norm(y, g, b):
    mu = jnp.mean(y, axis=-1, keepdims=True)
    yc = y - mu
    var = jnp.mean(yc * yc, axis=-1, keepdims=True)
    return yc * lax.rsqrt(var + LN_EPS) * g + b


def _proj_bias_kernel(x_ref, w_ref, b_ref, o_ref):
    acc = jnp.dot(x_ref[...].astype(bf16), w_ref[...], preferred_element_type=f32)
    o_ref[...] = (acc + b_ref[...]).astype(o_ref.dtype)


def _proj_kernel(x_ref, w_ref, o_ref):
    acc = jnp.dot(x_ref[...].astype(bf16), w_ref[...], preferred_element_type=f32)
    o_ref[...] = acc.astype(o_ref.dtype)


def _project(x, w, b, tm, tn, name):
    m, k = x.shape
    n = w.shape[1]
    tm = min(tm, m)
    in_specs = [pl.BlockSpec((tm, k), lambda i, j: (i, 0)),
                pl.BlockSpec((k, tn), lambda i, j: (0, j))]
    args = [x, w]
    body = _proj_kernel
    if b is not None:
        in_specs.append(pl.BlockSpec((1, tn), lambda i, j: (0, j)))
        args.append(b)
        body = _proj_bias_kernel
    return pl.pallas_call(
        body, grid=(m // tm, n // tn), in_specs=in_specs,
        out_specs=pl.BlockSpec((tm, tn), lambda i, j: (i, j)),
        out_shape=jax.ShapeDtypeStruct((m, n), bf16),
        compiler_params=_cparams(2), name=name)(*args)


def _conv_kernel(prev_ref, cur_ref, next_ref, w_ref, cb_ref, g_ref, b_ref, o_ref, h_s, *, tm, halo):
    i = pl.program_id(1)
    last = pl.num_programs(1) - 1

    def glu(u):
        u = u.astype(f32)
        return u[:, :BRANCH_W] * jax.nn.sigmoid(u[:, BRANCH_W:])

    h_s[0:halo, :] = jnp.where(i == 0, 0.0, glu(prev_ref[0]))
    h_s[halo:halo + tm, :] = glu(cur_ref[0])
    h_s[halo + tm:, :] = jnp.where(i == last, 0.0, glu(next_ref[0]))
    w = w_ref[...]
    acc = jnp.zeros((tm, BRANCH_W), f32) + cb_ref[...]
    base = halo - CONV_K // 2
    for k in range(CONV_K):
        acc = acc + h_s[base + k:base + k + tm, :] * w[k:k + 1, :]
    y = _layer_norm(acc, g_ref[...], b_ref[...])
    o_ref[0] = (y * jax.nn.sigmoid(y)).astype(o_ref.dtype)


def _conv_branch(u, conv_w, conv_b, ln_g, ln_b, tm):
    bsz, s, _ = u.shape
    halo = 16
    nb = tm // halo
    kern = functools.partial(_conv_kernel, tm=tm, halo=halo)
    full = lambda b, i: (0, 0)
    return pl.pallas_call(
        kern, grid=(bsz, s // tm),
        in_specs=[pl.BlockSpec((1, halo, 2 * BRANCH_W), lambda b, i: (b, jnp.maximum(i * nb - 1, 0), 0)),
                  pl.BlockSpec((1, tm, 2 * BRANCH_W), lambda b, i: (b, i, 0)),
                  pl.BlockSpec((1, halo, 2 * BRANCH_W),
                               lambda b, i: (b, jnp.minimum((i + 1) * nb, s // halo - 1), 0)),
                  pl.BlockSpec((CONV_K, BRANCH_W), full),
                  pl.BlockSpec((1, BRANCH_W), full),
                  pl.BlockSpec((1, BRANCH_W), full),
                  pl.BlockSpec((1, BRANCH_W), full)],
        out_specs=pl.BlockSpec((1, tm, BRANCH_W), lambda b, i: (b, i, 0)),
        out_shape=jax.ShapeDtypeStruct((bsz, s, BRANCH_W), bf16),
        scratch_shapes=[pltpu.VMEM((tm + 2 * halo, BRANCH_W), f32)],
        compiler_params=_cparams(2), name="conv_branch")(u, u, u, conv_w, conv_b, ln_g, ln_b)


def _softmax_chunk(s, c, vT_s, m_s, l_s, acc_s, const_bias=None):
    mc = jnp.max(s, axis=0, keepdims=True)
    if const_bias is not None:
        mc = mc + const_bias
    m_old = m_s[...]
    m_new = jnp.maximum(m_old, mc)
    alpha = jnp.exp(m_old - m_new)
    shift = m_new if const_bias is None else m_new - const_bias
    p = jnp.exp(s - shift)
    l_s[...] = alpha * l_s[...] + jnp.sum(p, axis=0, keepdims=True)
    acc_s[...] = alpha * acc_s[...] + jnp.dot(vT_s[c], p.astype(bf16), preferred_element_type=f32)
    m_s[...] = m_new


def _init_softmax_state(m_s, l_s, acc_s):
    m_s[...] = jnp.full(m_s.shape, NEG, f32)
    l_s[...] = jnp.zeros(l_s.shape, f32)
    acc_s[...] = jnp.zeros(acc_s.shape, f32)


def _build_vT(v_ref, sel, vT_s, n_chunks, tk):
    def body(c, carry):
        vc = v_ref[0, pl.ds(pl.multiple_of(c * tk, tk), tk), :]
        vT_s[c] = lax.dot_general(sel, vc, _NT, preferred_element_type=f32).astype(bf16)
        return carry
    lax.fori_loop(0, n_chunks, body, 0)


def _diff_kernel(q_ref, k_ref, v_ref, eq_ref, id_ref, idq_ref, wb_ref, cb_ref, lam_ref, g_ref, o_ref,
                 qT_s, vT_s, m_s, l_s, acc_s, *, seq, t, lam_init):
    qi = pl.program_id(2)
    n_chunks = seq // t

    @pl.when(qi == 0)
    def _():
        _build_vT(v_ref, id_ref[...], vT_s, n_chunks, t)

    q = q_ref[0]
    for m in range(2):
        qT_s[:, m * t:(m + 1) * t] = lax.dot_general(
            eq_ref[m], q, _NT, preferred_element_type=f32).astype(bf16)
    _init_softmax_state(m_s, l_s, acc_s)

    def scores(c):
        kc = k_ref[0, pl.ds(pl.multiple_of(c * t, t), t), :]
        return jnp.dot(kc, qT_s[...], preferred_element_type=f32)

    def far(side):
        def body(c, carry):
            _softmax_chunk(scores(c), c, vT_s, m_s, l_s, acc_s, const_bias=cb_ref[0, side])
            return carry
        return body

    def near(c, v):
        _softmax_chunk(scores(c) + wb_ref[0, v], c, vT_s, m_s, l_s, acc_s)

    lax.fori_loop(0, jnp.maximum(qi - 1, 0), far(0), 0)

    @pl.when(qi >= 1)
    def _():
        near(qi - 1, 0)

    near(qi, 1)

    @pl.when(qi + 1 < n_chunks)
    def _():
        near(qi + 1, 2)

    lax.fori_loop(qi + 2, n_chunks, far(1), 0)

    lam = lam_ref[...]
    lmb = (jnp.exp(jnp.sum(lam[0:1] * lam[1:2], axis=1, keepdims=True))
           - jnp.exp(jnp.sum(lam[2:3] * lam[3:4], axis=1, keepdims=True)) + lam_init)
    o_maps = acc_s[...] / l_s[...]
    o = o_maps[:, :t] - lmb * o_maps[:, t:]
    ms = jnp.mean(o * o, axis=0, keepdims=True)
    y = o * lax.rsqrt(ms + LN_EPS) * g_ref[...] * (1.0 - lam_init)
    o_ref[0] = lax.dot_general(idq_ref[...], y.astype(bf16), _NT,
                               preferred_element_type=f32).astype(o_ref.dtype)


def _diff_attention(u, eq, ident, identq, wb, cb, lam, sub_g, lam_init, t):
    bsz, s, _ = u.shape
    nq = 2 * t
    kern = functools.partial(_diff_kernel, seq=s, t=t, lam_init=lam_init)
    return pl.pallas_call(
        kern, grid=(DIFF_HEADS, bsz, s // t),
        in_specs=[pl.BlockSpec((1, t, LANE), lambda h, b, i: (b, i, _COL_BQ + h)),
                  pl.BlockSpec((1, s, LANE), lambda h, b, i: (b, 0, _COL_BK + h)),
                  pl.BlockSpec((1, s, LANE), lambda h, b, i: (b, 0, _COL_BV + h)),
                  pl.BlockSpec((2, LANE, LANE), lambda h, b, i: (0, 0, 0)),
                  pl.BlockSpec((LANE, LANE), lambda h, b, i: (0, 0)),
                  pl.BlockSpec((t, t), lambda h, b, i: (0, 0)),
                  pl.BlockSpec((1, 3, t, nq), lambda h, b, i: (h, 0, 0, 0)),
                  pl.BlockSpec((1, 2, 1, nq), lambda h, b, i: (h, 0, 0, 0)),
                  pl.BlockSpec((4, DIFF_DH), lambda h, b, i: (0, 0)),
                  pl.BlockSpec((DIFF_VD, 1), lambda h, b, i: (0, 0))],
        out_specs=pl.BlockSpec((1, t, LANE), lambda h, b, i: (b, i, h)),
        out_shape=jax.ShapeDtypeStruct((bsz, s, DIFF_HEADS * DIFF_VD), bf16),
        scratch_shapes=[pltpu.VMEM((LANE, nq), bf16),
                        pltpu.VMEM((s // t, DIFF_VD, t), bf16),
                        pltpu.VMEM((1, nq), f32),
                        pltpu.VMEM((1, nq), f32),
                        pltpu.VMEM((DIFF_VD, nq), f32)],
        compiler_params=_cparams(3), name="diff_attention")(
            u, u, u, eq, ident, identq, wb, cb, lam, sub_g)


def _stack_group_queries(q, e_ref, qT_s, tq):
    for g in range(4):
        qT_s[:, g * tq:(g + 1) * tq] = lax.dot_general(
            e_ref[0, g], q, _NT, preferred_element_type=f32).astype(bf16)


def _store_group_output(o, idq_ref, o_ref, tq):
    oT = jnp.concatenate([o[:, g * tq:(g + 1) * tq] for g in range(4)], axis=0).astype(bf16)
    o_ref[0] = lax.dot_general(idq_ref[...], oT, _NT, preferred_element_type=f32).astype(o_ref.dtype)


def _axial_kernel(q_ref, k_ref, v_ref, e_ref, ev_ref, idq_ref, o_ref,
                  qT_s, vT_s, m_s, l_s, acc_s, *, seq, tq, tk):
    qi = pl.program_id(2)
    n_chunks = seq // tk

    @pl.when(qi == 0)
    def _():
        _build_vT(v_ref, ev_ref[0], vT_s, n_chunks, tk)

    _stack_group_queries(q_ref[0], e_ref, qT_s, tq)
    _init_softmax_state(m_s, l_s, acc_s)

    def body(c, carry):
        kc = k_ref[0, pl.ds(pl.multiple_of(c * tk, tk), tk), :]
        s = jnp.dot(kc, qT_s[...], preferred_element_type=f32)
        _softmax_chunk(s, c, vT_s, m_s, l_s, acc_s)
        return carry
    lax.fori_loop(0, n_chunks, body, 0)
    _store_group_output(acc_s[...] / l_s[...], idq_ref, o_ref, tq)


def _axial_attention(q_rot, k_rot, u, e, ev, identq, tq, tk):
    bsz, s, _ = u.shape
    nq = 4 * tq
    kern = functools.partial(_axial_kernel, seq=s, tq=tq, tk=tk)
    return pl.pallas_call(
        kern, grid=(AX_KV, bsz, s // tq),
        in_specs=[pl.BlockSpec((1, tq, 2 * LANE), lambda h, b, i: (b, i, h)),
                  pl.BlockSpec((1, s, LANE), lambda h, b, i: (b, 0, 0)),
                  pl.BlockSpec((1, s, LANE), lambda h, b, i: (b, 0, _COL_DV)),
                  pl.BlockSpec((1, 4, LANE, 2 * LANE), lambda h, b, i: (h, 0, 0, 0)),
                  pl.BlockSpec((1, AX_DH, LANE), lambda h, b, i: (h, 0, 0)),
                  pl.BlockSpec((tq, tq), lambda h, b, i: (0, 0))],
        out_specs=pl.BlockSpec((1, tq, 2 * LANE), lambda h, b, i: (b, i, h)),
        out_shape=jax.ShapeDtypeStruct((bsz, s, AX_HEADS * AX_DH), bf16),
        scratch_shapes=[pltpu.VMEM((LANE, nq), bf16),
                        pltpu.VMEM((s // tk, AX_DH, tk), bf16),
                        pltpu.VMEM((1, nq), f32),
                        pltpu.VMEM((1, nq), f32),
                        pltpu.VMEM((AX_DH, nq), f32)],
        compiler_params=_cparams(3), name="axial_attention")(q_rot, k_rot, u, e, ev, identq)


def _rope_kernel(qa_ref, qb_ref, k_ref, cos_ref, sin_ref, qg_ref, kg_ref, qo_ref, ko_ref):
    lane = lax.broadcasted_iota(jnp.int32, (1, LANE), 1)
    low_head = lane < AX_DH
    first_half = (lane % (AX_DH // 2)) < (AX_DH // 4)
    cos = cos_ref[...]
    sin = sin_ref[...]

    def norm_rope(x, g):
        x = x.astype(f32)
        sq = x * x
        ms_lo = jnp.sum(jnp.where(low_head, sq, 0.0), axis=1, keepdims=True) * (1.0 / AX_DH)
        ms_hi = jnp.sum(jnp.where(low_head, 0.0, sq), axis=1, keepdims=True) * (1.0 / AX_DH)
        r = jnp.where(low_head, lax.rsqrt(ms_lo + LN_EPS), lax.rsqrt(ms_hi + LN_EPS))
        xn = x * r * g
        partner = jnp.where(first_half, pltpu.roll(xn, LANE - AX_DH // 4, axis=1),
                            pltpu.roll(xn, AX_DH // 4, axis=1))
        return xn * cos + partner * sin

    qg = qg_ref[...]
    qa = qa_ref[...]
    qb = qb_ref[...]
    for j in range(2):
        qo_ref[:, j * LANE:(j + 1) * LANE] = norm_rope(qa[:, j * LANE:(j + 1) * LANE], qg).astype(bf16)
        qo_ref[:, (2 + j) * LANE:(3 + j) * LANE] = norm_rope(qb[:, j * LANE:(j + 1) * LANE], qg).astype(bf16)
    ko_ref[...] = norm_rope(k_ref[...], kg_ref[...]).astype(bf16)


def _axial_rope(u2, cos_t, sin_t, qn_g, kn_g, seq, tm):
    t = u2.shape[0]
    n_pos = seq // tm
    return pl.pallas_call(
        _rope_kernel, grid=(t // tm,),
        in_specs=[pl.BlockSpec((tm, 2 * LANE), lambda i: (i, _COL_DQ // 2)),
                  pl.BlockSpec((tm, 2 * LANE), lambda i: (i, _COL_DQ // 2 + 1)),
                  pl.BlockSpec((tm, LANE), lambda i: (i, _COL_DK)),
                  pl.BlockSpec((tm, LANE), lambda i: (i % n_pos, 0)),
                  pl.BlockSpec((tm, LANE), lambda i: (i % n_pos, 0)),
                  pl.BlockSpec((1, LANE), lambda i: (0, 0)),
                  pl.BlockSpec((1, LANE), lambda i: (0, 0))],
        out_specs=[pl.BlockSpec((tm, AX_HEADS * AX_DH), lambda i: (i, 0)),
                   pl.BlockSpec((tm, LANE), lambda i: (i, 0))],
        out_shape=[jax.ShapeDtypeStruct((t, AX_HEADS * AX_DH), bf16),
                   jax.ShapeDtypeStruct((t, AX_KV * AX_DH), bf16)],
        compiler_params=_cparams(1), name="axial_rope")(u2, u2, u2, cos_t, sin_t, qn_g, kn_g)


def _window_kernel(q_ref, k_ref, v_ref, e_ref, ev_ref, idq_ref, bw_ref, sink_ref, o_ref,
                   qT_s, vT_s, *, seq, tq, span):
    qi = pl.program_id(2)

    @pl.when(qi == 0)
    def _():
        _build_vT(v_ref, ev_ref[0], vT_s, seq // LANE, LANE)

    _stack_group_queries(q_ref[0], e_ref, qT_s, tq)
    start = pl.multiple_of(jnp.clip(qi * tq - WIN, 0, seq - span), LANE)
    kw = k_ref[0, pl.ds(start, span), :]
    s = jnp.dot(kw, qT_s[...], preferred_element_type=f32) + bw_ref[0, 0]
    sink = sink_ref[0]
    m = jnp.maximum(jnp.max(s, axis=0, keepdims=True), sink)
    p = jnp.exp(s - m)
    l = jnp.sum(p, axis=0, keepdims=True) + jnp.exp(sink - m)
    pb = p.astype(bf16)
    c0 = start // LANE
    acc = jnp.dot(vT_s[c0], pb[0:LANE], preferred_element_type=f32)
    for j in range(1, span // LANE):
        acc = acc + jnp.dot(vT_s[c0 + j], pb[j * LANE:(j + 1) * LANE], preferred_element_type=f32)
    _store_group_output(acc / l, idq_ref, o_ref, tq)


def _window_attention(u, e, ev, identq, bw, sink, tq, span):
    bsz, s, _ = u.shape
    nq = 4 * tq
    n_tiles = s // tq
    kern = functools.partial(_window_kernel, seq=s, tq=tq, span=span)
    variant = lambda i: jnp.where(i == 0, 0, jnp.where(i == n_tiles - 1, 2, 1))
    return pl.pallas_call(
        kern, grid=(WIN_KV, bsz, n_tiles),
        in_specs=[pl.BlockSpec((1, tq, 2 * LANE), lambda h, b, i: (b, i, _COL_CQ // 2 + h)),
                  pl.BlockSpec((1, s, LANE), lambda h, b, i: (b, 0, _COL_CK)),
                  pl.BlockSpec((1, s, LANE), lambda h, b, i: (b, 0, _COL_CV)),
                  pl.BlockSpec((1, 4, LANE, 2 * LANE), lambda h, b, i: (h, 0, 0, 0)),
                  pl.BlockSpec((1, WIN_DH, LANE), lambda h, b, i: (h, 0, 0)),
                  pl.BlockSpec((tq, tq), lambda h, b, i: (0, 0)),
                  pl.BlockSpec((1, 1, span, nq), lambda h, b, i: (variant(i), h, 0, 0)),
                  pl.BlockSpec((1, 1, nq), lambda h, b, i: (h, 0, 0))],
        out_specs=pl.BlockSpec((1, tq, 2 * LANE), lambda h, b, i: (b, i, h)),
        out_shape=jax.ShapeDtypeStruct((bsz, s, WIN_HEADS * WIN_DH), bf16),
        scratch_shapes=[pltpu.VMEM((LANE, nq), bf16),
                        pltpu.VMEM((s // LANE, WIN_DH, LANE), bf16)],
        compiler_params=_cparams(3), name="window_attention")(u, u, u, e, ev, identq, bw, sink)


def _merge_kernel(oa_ref, ob_ref, oc_ref, od_ref, gl_ref, x_ref, wb_ref, wo_ref, g_ref, b_ref, o_ref):
    merged = None
    for n, o_n in enumerate((oa_ref, ob_ref, oc_ref, od_ref)):
        gate = jax.nn.sigmoid(gl_ref[:, n * D_MODEL:(n + 1) * D_MODEL].astype(f32))
        term = gate * jnp.dot(o_n[...], wb_ref[n], preferred_element_type=f32)
        merged = term if merged is None else merged + term
    h = jnp.dot(merged.astype(bf16), wo_ref[...], preferred_element_type=f32)
    o_ref[...] = _layer_norm(DN_ALPHA * x_ref[...] + h, g_ref[...], b_ref[...])


def _merge(o_a, o_b, o_c, o_d, u2, x2, w_branch, w_out, ln_g, ln_b, tm):
    t = x2.shape[0]
    row = lambda i: (i, 0)
    full2 = lambda i: (0, 0)
    return pl.pallas_call(
        _merge_kernel, grid=(t // tm,),
        in_specs=[pl.BlockSpec((tm, BRANCH_W), row)] * 4 + [
            pl.BlockSpec((tm, N_BRANCH * D_MODEL), lambda i: (i, 1)),
            pl.BlockSpec((tm, D_MODEL), row),
            pl.BlockSpec((N_BRANCH, BRANCH_W, D_MODEL), lambda i: (0, 0, 0)),
            pl.BlockSpec((D_MODEL, D_MODEL), full2),
            pl.BlockSpec((1, D_MODEL), full2),
            pl.BlockSpec((1, D_MODEL), full2)],
        out_specs=pl.BlockSpec((tm, D_MODEL), row),
        out_shape=jax.ShapeDtypeStruct((t, D_MODEL), f32),
        compiler_params=_cparams(1), name="merge")(o_a, o_b, o_c, o_d, u2, x2, w_branch, w_out, ln_g, ln_b)


def _cross_kernel(x_ref, kv_ref, wq_ref, wo_ref, g_ref, b_ref, o_ref, oc_s):
    x = x_ref[0]
    q = jnp.dot(x.astype(bf16), wq_ref[...], preferred_element_type=f32) * (X_DH ** -0.5)
    q = q.astype(bf16)
    for h in range(X_HEADS):
        kh = kv_ref[0, :, h * X_DH:(h + 1) * X_DH]
        vh = kv_ref[0, :, D_MODEL + h * X_DH:D_MODEL + (h + 1) * X_DH]
        s = lax.dot_general(q[:, h * X_DH:(h + 1) * X_DH], kh, _NT, preferred_element_type=f32)
        m = jnp.max(s, axis=1, keepdims=True)
        p = jnp.exp(s - m)
        l = jnp.sum(p, axis=1, keepdims=True)
        oh = jnp.dot(p.astype(bf16), vh, preferred_element_type=f32) / l
        oc_s[:, h * X_DH:(h + 1) * X_DH] = oh.astype(bf16)
    hcross = jnp.dot(oc_s[...], wo_ref[...], preferred_element_type=f32)
    o_ref[0] = _layer_norm(DN_ALPHA * x + hcross, g_ref[...], b_ref[...])


def _cross_attention(x, kv, w_q, w_o, ln_g, ln_b, tm):
    bsz, s, _ = x.shape
    mlen = kv.shape[1]
    full2 = lambda b, i: (0, 0)
    return pl.pallas_call(
        _cross_kernel, grid=(bsz, s // tm),
        in_specs=[pl.BlockSpec((1, tm, D_MODEL), lambda b, i: (b, i, 0)),
                  pl.BlockSpec((1, mlen, 2 * D_MODEL), lambda b, i: (b, 0, 0)),
                  pl.BlockSpec((D_MODEL, D_MODEL), full2),
                  pl.BlockSpec((D_MODEL, D_MODEL), full2),
                  pl.BlockSpec((1, D_MODEL), full2),
                  pl.BlockSpec((1, D_MODEL), full2)],
        out_specs=pl.BlockSpec((1, tm, D_MODEL), lambda b, i: (b, i, 0)),
        out_shape=jax.ShapeDtypeStruct((bsz, s, D_MODEL), f32),
        scratch_shapes=[pltpu.VMEM((tm, D_MODEL), bf16)],
        compiler_params=_cparams(2), name="cross_attention")(x, kv, w_q, w_o, ln_g, ln_b)


def _ffn_kernel(prev_ref, cur_ref, next_ref, wg_ref, wu_ref, cwg_ref, cwu_ref, cbg_ref, cbu_ref,
                wd_ref, g_ref, b_ref, o_ref, a_s, acc_s, *, tm, halo, sub):
    i = pl.program_id(1)
    f = pl.program_id(2)
    last = pl.num_programs(1) - 1
    x = cur_ref[0]
    xp = jnp.where(i == 0, 0.0, prev_ref[0])
    xn = jnp.where(i == last, 0.0, next_ref[0])
    xb = jnp.concatenate([xp, x, xn], axis=0).astype(bf16)
    rows = tm + 2 * halo

    def conv3(h, w, b):
        below = pltpu.roll(h, 1, axis=0)
        above = pltpu.roll(h, rows - 1, axis=0)
        y = below * w[0:1] + h * w[1:2] + above * w[2:3] + b
        return y[halo:halo + tm]

    for j in range(0, FFN_CHUNK, sub):
        wdt = min(sub, FFN_CHUNK - j)
        hg = jnp.dot(xb, wg_ref[:, j:j + wdt], preferred_element_type=f32)
        hu = jnp.dot(xb, wu_ref[:, j:j + wdt], preferred_element_type=f32)
        yg = conv3(hg, cwg_ref[:, j:j + wdt], cbg_ref[:, j:j + wdt])
        yu = conv3(hu, cwu_ref[:, j:j + wdt], cbu_ref[:, j:j + wdt])
        act = 0.5 * yg * (1.0 + lax.erf(yg * (2.0 ** -0.5)))
        a_s[:, j:j + wdt] = (act * yu).astype(bf16)
    part = jnp.dot(a_s[...], wd_ref[...], preferred_element_type=f32)

    @pl.when(f == 0)
    def _():
        acc_s[...] = part

    @pl.when(f == pl.num_programs(2) - 1)
    def _():
        o_ref[0] = _layer_norm(DN_ALPHA * x + acc_s[...] + part, g_ref[...], b_ref[...])


def _conv_ffn(x, wg, wu, cwg, cwu, cbg, cbu, wd, ln_g, ln_b, tm):
    bsz, s, _ = x.shape
    halo = 8
    nb = tm // halo
    n_f = FFN_PAD // FFN_CHUNK
    assert n_f == 2
    kern = functools.partial(_ffn_kernel, tm=tm, halo=halo, sub=2 * LANE)
    fcol = lambda b, i, f: (0, f)
    full2 = lambda b, i, f: (0, 0)
    return pl.pallas_call(
        kern, grid=(bsz, s // tm, n_f),
        in_specs=[pl.BlockSpec((1, halo, D_MODEL), lambda b, i, f: (b, jnp.maximum(i * nb - 1, 0), 0)),
                  pl.BlockSpec((1, tm, D_MODEL), lambda b, i, f: (b, i, 0)),
                  pl.BlockSpec((1, halo, D_MODEL),
                               lambda b, i, f: (b, jnp.minimum((i + 1) * nb, s // halo - 1), 0)),
                  pl.BlockSpec((D_MODEL, FFN_CHUNK), fcol),
                  pl.BlockSpec((D_MODEL, FFN_CHUNK), fcol),
                  pl.BlockSpec((FFN_CONV_K, FFN_CHUNK), fcol),
                  pl.BlockSpec((FFN_CONV_K, FFN_CHUNK), fcol),
                  pl.BlockSpec((1, FFN_CHUNK), fcol),
                  pl.BlockSpec((1, FFN_CHUNK), fcol),
                  pl.BlockSpec((FFN_CHUNK, D_MODEL), lambda b, i, f: (f, 0)),
                  pl.BlockSpec((1, D_MODEL), full2),
                  pl.BlockSpec((1, D_MODEL), full2)],
        out_specs=pl.BlockSpec((1, tm, D_MODEL), lambda b, i, f: (b, i, 0)),
        out_shape=jax.ShapeDtypeStruct((bsz, s, D_MODEL), f32),
        scratch_shapes=[pltpu.VMEM((tm, FFN_CHUNK), bf16),
                        pltpu.VMEM((tm, D_MODEL), f32)],
        compiler_params=_cparams(3), name="conv_ffn")(
            x, x, x, wg, wu, cwg, cwu, cbg, cbu, wd, ln_g, ln_b)


def _t5_bucket(rel):
    half = NUM_BUCKETS // 2
    max_exact = half // 2
    n = jnp.abs(rel)
    nf = jnp.maximum(n, 1).astype(jnp.float32)
    large = max_exact + (jnp.log(nf / max_exact) / math.log(MAX_DISTANCE / max_exact)
                         * (half - max_exact)).astype(jnp.int32)
    large = jnp.minimum(large, half - 1)
    return jnp.where(rel > 0, half, 0) + jnp.where(n < max_exact, n, large)


def _diff_bias_tiles(rel_bias, t):
    table = rel_bias[:, :2 * DIFF_HEADS].reshape(NUM_BUCKETS, DIFF_HEADS, 2).astype(f32)
    kk = jnp.arange(t)[:, None]
    qq = jnp.arange(t)[None, :]
    tiles = []
    for v in range(3):
        bias = table[_t5_bucket((v - 1) * t + kk - qq)]
        tiles.append(jnp.transpose(bias, (2, 0, 3, 1)).reshape(DIFF_HEADS, t, 2 * t))
    wb = jnp.stack(tiles, axis=1)
    far = table[_t5_bucket(jnp.array([-(t + 1), t + 1]))]
    cb = jnp.repeat(jnp.transpose(far, (1, 0, 2)), t, axis=-1).reshape(DIFF_HEADS, 2, 1, 2 * t)
    return wb, cb


def _window_bias_tiles(rel_bias, tq, span):
    table = rel_bias[:, 2 * DIFF_HEADS:].reshape(NUM_BUCKETS, WIN_KV, WIN_HEADS // WIN_KV).astype(f32)
    kk = jnp.arange(span)[:, None]
    qq = jnp.arange(tq)[None, :]
    tiles = []
    for start in (0, -WIN, tq - span):
        rel = start + kk - qq
        bias = jnp.where((jnp.abs(rel) <= WIN)[:, :, None, None], table[_t5_bucket(rel)], NEG)
        tiles.append(jnp.transpose(bias, (2, 0, 3, 1)).reshape(WIN_KV, span, 4 * tq))
    return jnp.stack(tiles, axis=0)


def _rope_tables(seq):
    rows = seq // GRID_W
    row = jnp.repeat(jnp.arange(rows, dtype=f32), GRID_W)
    col = jnp.tile(jnp.arange(GRID_W, dtype=f32), rows)
    n_freq = AX_DH // 4
    inv = ROPE_THETA ** (-jnp.arange(n_freq, dtype=f32) / n_freq)
    ang_r = row[:, None] * inv
    ang_c = col[:, None] * inv
    cos_h = jnp.concatenate([jnp.cos(ang_r), jnp.cos(ang_r), jnp.cos(ang_c), jnp.cos(ang_c)], -1)
    sin_h = jnp.concatenate([-jnp.sin(ang_r), jnp.sin(ang_r), -jnp.sin(ang_c), jnp.sin(ang_c)], -1)
    return jnp.tile(cos_h, (1, 2)), jnp.tile(sin_h, (1, 2))


def _selectors():
    r = jnp.arange(LANE)[:, None]
    c = jnp.arange(LANE)[None, :]
    eq = jnp.stack([jnp.where((r == c) & (r // DIFF_DH == m), DIFF_DH ** -0.5, 0.0) for m in range(2)])
    c2 = jnp.arange(2 * LANE)[None, :]
    e = jnp.stack([jnp.stack([jnp.where((r // WIN_DH == kv) & (c2 // WIN_DH == g)
                                        & (r % WIN_DH == c2 % WIN_DH), WIN_DH ** -0.5, 0.0)
                              for g in range(4)]) for kv in range(2)])
    d = jnp.arange(WIN_DH)[:, None]
    ev = jnp.stack([jnp.where(c == kv * WIN_DH + d, 1.0, 0.0) for kv in range(2)])
    return eq.astype(bf16), e.astype(bf16), ev.astype(bf16)


def _encode(x, mem, p, tables):
    bsz, s, _ = x.shape
    t_rows = bsz * s
    eq, e, ev, wb, cb, bw = tables
    ident = jnp.eye(LANE, dtype=bf16)
    ident256 = jnp.eye(2 * LANE, dtype=bf16)
    ident512 = jnp.eye(4 * LANE, dtype=bf16)
    cos_t, sin_t = _rope_tables(s)
    for l in range(DEPTH):
        lam_init = 0.8 - 0.6 * math.exp(-0.3 * l)
        x2 = x.reshape(t_rows, D_MODEL)
        u2 = _project(x2, p['w_in'][l], p['b_in'][l], 1024, 1024, "in_proj")
        u = u2.reshape(bsz, s, N_IN)
        o_a = _conv_branch(u, p['a_conv_w'][l], p['a_conv_b'][l], p['a_ln_g'][l], p['a_ln_b'][l], 512)
        o_b = _diff_attention(u, eq, ident, ident512, wb, cb, p['diff_lam'][l], p['diff_sub_g'][l],
                              lam_init, 512)
        o_c = _window_attention(u, e, ev, ident256, bw, p['win_sink'][l], 256, 512)
        q_rot, k_rot = _axial_rope(u2, cos_t, sin_t, p['ax_qn_g'][l], p['ax_kn_g'][l], s, 1024)
        o_d = _axial_attention(q_rot.reshape(bsz, s, -1), k_rot.reshape(bsz, s, -1), u, e, ev,
                               ident256, 256, 512)
        x2 = _merge(o_a.reshape(t_rows, -1), o_b.reshape(t_rows, -1), o_c.reshape(t_rows, -1),
                    o_d.reshape(t_rows, -1), u2, x2, p['w_branch'][l], p['w_mix_out'][l],
                    p['ln1_g'][l], p['ln1_b'][l], 512)
        kv = _project(mem.reshape(-1, D_MODEL), p['w_xkv'][l], None, 1024, 1024, "mem_proj")
        x = _cross_attention(x2.reshape(bsz, s, D_MODEL), kv.reshape(bsz, -1, 2 * D_MODEL),
                             p['w_xq'][l], p['w_xo'][l], p['ln2_g'][l], p['ln2_b'][l], 512)
        x = _conv_ffn(x, p['wg'][l], p['wu'][l], p['cwg'][l], p['cwu'][l], p['cbg'][l], p['cbu'][l],
                      p['wd'][l], p['ln3_g'][l], p['ln3_b'][l], 512)
    return x


def _pad_last(a, n):
    return jnp.pad(a, [(0, 0)] * (a.ndim - 1) + [(0, n - a.shape[-1])])


def kernel(x_prompt, x_sample, mem_prompt, mem_sample, rel_bias, w_in, b_in, a_conv_w, a_conv_b, a_ln_g,
           a_ln_b, diff_lam, diff_sub_g, win_sink, ax_qn_g, ax_kn_g, w_branch, w_mix_out, ln1_g, ln1_b,
           w_xq, w_xkv, w_xo, ln2_g, ln2_b, w_up, f_conv_w, f_conv_b, w_down, ln3_g, ln3_b):
    depth = w_in.shape[0]
    row = lambda a: a.reshape(depth, 1, a.shape[-1])
    tq_win = 256
    p = {
        'w_in': w_in.astype(bf16), 'b_in': row(b_in),
        'a_conv_w': a_conv_w, 'a_conv_b': row(a_conv_b), 'a_ln_g': row(a_ln_g), 'a_ln_b': row(a_ln_b),
        'diff_lam': diff_lam, 'diff_sub_g': diff_sub_g.reshape(depth, DIFF_VD, 1),
        'win_sink': jnp.repeat(win_sink.reshape(depth, WIN_KV, 1, WIN_HEADS // WIN_KV), tq_win, axis=-1),
        'ax_qn_g': jnp.tile(row(ax_qn_g), (1, 1, 2)), 'ax_kn_g': jnp.tile(row(ax_kn_g), (1, 1, 2)),
        'w_branch': w_branch.astype(bf16), 'w_mix_out': w_mix_out.astype(bf16),
        'ln1_g': row(ln1_g), 'ln1_b': row(ln1_b),
        'w_xq': w_xq.astype(bf16), 'w_xkv': w_xkv.astype(bf16), 'w_xo': w_xo.astype(bf16),
        'ln2_g': row(ln2_g), 'ln2_b': row(ln2_b),
        'wg': _pad_last(w_up[..., :FFN_DIM], FFN_PAD).astype(bf16),
        'wu': _pad_last(w_up[..., FFN_DIM:], FFN_PAD).astype(bf16),
        'cwg': _pad_last(f_conv_w[..., :FFN_DIM], FFN_PAD), 'cwu': _pad_last(f_conv_w[..., FFN_DIM:], FFN_PAD),
        'cbg': _pad_last(row(f_conv_b[..., :FFN_DIM]), FFN_PAD),
        'cbu': _pad_last(row(f_conv_b[..., FFN_DIM:]), FFN_PAD),
        'wd': jnp.pad(w_down, ((0, 0), (0, FFN_PAD - FFN_DIM), (0, 0))).astype(bf16),
        'ln3_g': row(ln3_g), 'ln3_b': row(ln3_b),
    }
    eq, e, ev = _selectors()
    wb, cb = _diff_bias_tiles(rel_bias, 512)
    bw = _window_bias_tiles(rel_bias, tq_win, 512)
    tables = (eq, e, ev, wb, cb, bw)
    return (_encode(x_prompt, mem_prompt, p, tables), _encode(x_sample, mem_sample, p, tables))
```

```python
import functools
import math

import jax
import jax.numpy as jnp
from jax import lax
from jax.experimental import pallas as pl
from jax.experimental.pallas import tpu as pltpu

f32 = jnp.float32
bf16 = jnp.bfloat16

D_MODEL = 1024
DEPTH = 2
GRID_W = 64
N_BRANCH = 4
BRANCH_W = 512
CONV_K = 31
DIFF_HEADS = 4
DIFF_DH = 64
DIFF_VD = 2 * DIFF_DH
WIN = 128
WIN_HEADS = 8
WIN_KV = 2
WIN_DH = 64
AX_HEADS = 8
AX_KV = 2
AX_DH = 64
ROPE_THETA = 10000.0
X_HEADS = 4
X_DH = D_MODEL // X_HEADS
FFN_DIM = 2752
FFN_CONV_K = 3
NUM_BUCKETS = 32
MAX_DISTANCE = 128
LN_EPS = 1e-5
NEG = -1e30
DN_ALPHA = (2 * DEPTH) ** 0.25
LOG2E = math.log2(math.e)
N_IN = 8192

_COL_UA = 0
_COL_BQ = 8
_COL_BK = 12
_COL_BV = 16
_COL_CQ = 20
_COL_CK = 24
_COL_CV = 25
_COL_DQ = 26
_COL_DK = 30
_COL_DV = 31
_COL_GL = 32

LANE = 128
FFN_PAD = 2816
FFN_CHUNK = FFN_PAD // 2
VMEM_LIMIT = 48 * 1024 * 1024

_NT = (((1,), (1,)), ((), ()))


def _cparams(n_axes):
    return pltpu.CompilerParams(dimension_semantics=("arbitrary",) * n_axes,
                                vmem_limit_bytes=VMEM_LIMIT)


def _layer_norm(y, g, b):
    mu = jnp.mean(y, axis=-1, keepdims=True)
    yc = y - mu
    var = jnp.mean(yc * yc, axis=-1, keepdims=True)
    return yc * lax.rsqrt(var + LN_EPS) * g + b


def _proj_bias_kernel(x_ref, w_ref, b_ref, o_ref):
    acc = jnp.dot(x_ref[...].astype(bf16), w_ref[...], preferred_element_type=f32)
    o_ref[...] = (acc + b_ref[...]).astype(o_ref.dtype)


def _proj_kernel(x_ref, w_ref, o_ref):
    acc = jnp.dot(x_ref[...].astype(bf16), w_ref[...], preferred_element_type=f32)
    o_ref[...] = acc.astype(o_ref.dtype)


def _project(x, w, b, tm, tn, name):
    m, k = x.shape
    n = w.shape[1]
    tm = min(tm, m)
    in_specs = [pl.BlockSpec((tm, k), lambda i, j: (i, 0)),
                pl.BlockSpec((k, tn), lambda i, j: (0, j))]
    args = [x, w]
    body = _proj_kernel
    if b is not None:
        in_specs.append(pl.BlockSpec((1, tn), lambda i, j: (0, j)))
        args.append(b)
        body = _proj_bias_kernel
    return pl.pallas_call(
        body, grid=(m // tm, n // tn), in_specs=in_specs,
        out_specs=pl.BlockSpec((tm, tn), lambda i, j: (i, j)),
        out_shape=jax.ShapeDtypeStruct((m, n), bf16),
        compiler_params=_cparams(2), name=name)(*args)


def _conv_kernel(prev_ref, cur_ref, next_ref, w_ref, cb_ref, g_ref, b_ref, o_ref, h_s, *, tm, halo):
    i = pl.program_id(1)
    last = pl.num_programs(1) - 1

    def glu(u):
        u = u.astype(f32)
        return u[:, :BRANCH_W] * jax.nn.sigmoid(u[:, BRANCH_W:])

    h_s[0:halo, :] = jnp.where(i == 0, 0.0, glu(prev_ref[0]))
    h_s[halo:halo + tm, :] = glu(cur_ref[0])
    h_s[halo + tm:, :] = jnp.where(i == last, 0.0, glu(next_ref[0]))
    w = w_ref[...]
    acc = jnp.zeros((tm, BRANCH_W), f32) + cb_ref[...]
    base = halo - CONV_K // 2
    for k in range(CONV_K):
        acc = acc + h_s[base + k:base + k + tm, :] * w[k:k + 1, :]
    y = _layer_norm(acc, g_ref[...], b_ref[...])
    o_ref[0] = (y * jax.nn.sigmoid(y)).astype(o_ref.dtype)


def _conv_branch(u, conv_w, conv_b, ln_g, ln_b, tm):
    bsz, s, _ = u.shape
    halo = 16
    nb = tm // halo
    kern = functools.partial(_conv_kernel, tm=tm, halo=halo)
    full = lambda b, i: (0, 0)
    return pl.pallas_call(
        kern, grid=(bsz, s // tm),
        in_specs=[pl.BlockSpec((1, halo, 2 * BRANCH_W), lambda b, i: (b, jnp.maximum(i * nb - 1, 0), 0)),
                  pl.BlockSpec((1, tm, 2 * BRANCH_W), lambda b, i: (b, i, 0)),
                  pl.BlockSpec((1, halo, 2 * BRANCH_W),
                               lambda b, i: (b, jnp.minimum((i + 1) * nb, s // halo - 1), 0)),
                  pl.BlockSpec((CONV_K, BRANCH_W), full),
                  pl.BlockSpec((1, BRANCH_W), full),
                  pl.BlockSpec((1, BRANCH_W), full),
                  pl.BlockSpec((1, BRANCH_W), full)],
        out_specs=pl.BlockSpec((1, tm, BRANCH_W), lambda b, i: (b, i, 0)),
        out_shape=jax.ShapeDtypeStruct((bsz, s, BRANCH_W), bf16),
        scratch_shapes=[pltpu.VMEM((tm + 2 * halo, BRANCH_W), f32)],
        compiler_params=_cparams(2), name="conv_branch")(u, u, u, conv_w, conv_b, ln_g, ln_b)


COL_BLOCK = 2 * LANE


def _score_pass(kc, qT_s, s_s, bias_ref=None):
    nq = qT_s.shape[1]
    maxes = []
    for j in range(0, nq, COL_BLOCK):
        s = jnp.dot(kc, qT_s[:, j:j + COL_BLOCK], preferred_element_type=f32)
        if bias_ref is not None:
            s = s + bias_ref[:, j:j + COL_BLOCK]
        s_s[:, j:j + COL_BLOCK] = s
        maxes.append(jnp.max(s, axis=0, keepdims=True))
    return jnp.concatenate(maxes, axis=1)


def _exp2_pass(s_s, p_s, shift):
    tk, nq = s_s.shape
    shift16 = jnp.broadcast_to(shift, (16, nq))
    for r in range(0, tk, 16):
        p_s[r:r + 16, :] = jnp.exp2(s_s[r:r + 16, :] - shift16).astype(bf16)


ONES_ROWS = 16


class _SoftmaxState:
    def __init__(self, s0, s1, p0, p1, mc, m, alpha, acc):
        self.s = (s0, s1)
        self.p = (p0, p1)
        self.mc, self.m, self.alpha, self.acc = mc, m, alpha, acc

    @staticmethod
    def scratch_shapes(tk, nq, dv):
        return [pltpu.VMEM((tk, nq), f32), pltpu.VMEM((tk, nq), f32),
                pltpu.VMEM((tk, nq), bf16), pltpu.VMEM((tk, nq), bf16),
                pltpu.VMEM((1, nq), f32), pltpu.VMEM((1, nq), f32), pltpu.VMEM((1, nq), f32),
                pltpu.VMEM((dv + ONES_ROWS, nq), f32)]


def _attention_chunks(n_chunks, k_chunk, bias_tile, qT_s, vT_s, st):
    st.m[...] = jnp.full(st.m.shape, NEG, f32)
    st.acc[...] = jnp.zeros(st.acc.shape, f32)

    def score(c, buf):
        st.mc[...] = _score_pass(k_chunk(c), qT_s, st.s[buf], bias_tile(c))

    def softmax(buf):
        m_old = st.m[...]
        m_new = jnp.maximum(m_old, st.mc[...])
        _exp2_pass(st.s[buf], st.p[buf], m_new)
        st.alpha[...] = jnp.exp2(m_old - m_new)
        st.m[...] = m_new

    def pv(c, buf):
        st.acc[...] = st.alpha[...] * st.acc[...] + jnp.dot(vT_s[c], st.p[buf][...],
                                                            preferred_element_type=f32)

    def step(c, buf):
        pv(c - 1, 1 - buf)
        softmax(buf)
        score(c + 1, 1 - buf)

    score(0, 0)
    softmax(0)
    score(1, 1)

    def pair(j, carry):
        c = 2 * j + 1
        step(c, 1)
        step(c + 1, 0)
        return carry
    lax.fori_loop(0, (n_chunks - 2) // 2, pair, 0)
    pv(n_chunks - 2, 0)
    softmax(1)
    pv(n_chunks - 1, 1)


def _build_vT(v_ref, sel, vT_s, n_chunks, tk):
    dv = sel.shape[0]

    def body(c, carry):
        vc = v_ref[0, pl.ds(pl.multiple_of(c * tk, tk), tk), :]
        vT_s[c, 0:dv, :] = lax.dot_general(sel, vc, _NT, preferred_element_type=f32).astype(bf16)
        vT_s[c, dv:, :] = jnp.ones((ONES_ROWS, tk), bf16)
        return carry
    lax.fori_loop(0, n_chunks, body, 0)


def _diff_kernel(q_ref, k_ref, v_ref, eq_ref, id_ref, idq_ref, wb_ref, lam_ref, g_ref, o_ref,
                 qT_s, vT_s, *softmax_scratch, seq, t, lam_init):
    qi = pl.program_id(2)
    n_chunks = seq // t
    st = _SoftmaxState(*softmax_scratch)

    @pl.when(qi == 0)
    def _():
        _build_vT(v_ref, id_ref[...], vT_s, n_chunks, t)

    q = q_ref[0]
    for m in range(2):
        qT_s[:, m * t:(m + 1) * t] = (lax.dot_general(
            eq_ref[m], q, _NT, preferred_element_type=f32) * (DIFF_DH ** -0.5 * LOG2E)).astype(bf16)

    def k_chunk(c):
        return k_ref[0, pl.ds(pl.multiple_of(c * t, t), t), :]

    def bias_tile(c):
        return wb_ref.at[0, jnp.clip(c - qi, -2, 2) + 2]

    _attention_chunks(n_chunks, k_chunk, bias_tile, qT_s, vT_s, st)

    lam = lam_ref[...]
    lmb = (jnp.exp(jnp.sum(lam[0:1] * lam[1:2], axis=1, keepdims=True))
           - jnp.exp(jnp.sum(lam[2:3] * lam[3:4], axis=1, keepdims=True)) + lam_init)
    o_maps = st.acc[0:DIFF_VD, :] / st.acc[DIFF_VD:DIFF_VD + 1, :]
    o = o_maps[:, :t] - lmb * o_maps[:, t:]
    ms = jnp.mean(o * o, axis=0, keepdims=True)
    y = o * lax.rsqrt(ms + LN_EPS) * g_ref[...] * (1.0 - lam_init)
    o_ref[0] = lax.dot_general(idq_ref[...], y.astype(bf16), _NT,
                               preferred_element_type=f32).astype(o_ref.dtype)


def _diff_attention(u, eq, ident, identq, wb, lam, sub_g, lam_init, t):
    bsz, s, _ = u.shape
    nq = 2 * t
    kern = functools.partial(_diff_kernel, seq=s, t=t, lam_init=lam_init)
    return pl.pallas_call(
        kern, grid=(DIFF_HEADS, bsz, s // t),
        in_specs=[pl.BlockSpec((1, t, LANE), lambda h, b, i: (b, i, _COL_BQ + h)),
                  pl.BlockSpec((1, s, LANE), lambda h, b, i: (b, 0, _COL_BK + h)),
                  pl.BlockSpec((1, s, LANE), lambda h, b, i: (b, 0, _COL_BV + h)),
                  pl.BlockSpec((2, LANE, LANE), lambda h, b, i: (0, 0, 0)),
                  pl.BlockSpec((LANE, LANE), lambda h, b, i: (0, 0)),
                  pl.BlockSpec((t, t), lambda h, b, i: (0, 0)),
                  pl.BlockSpec((1, 5, t, nq), lambda h, b, i: (h, 0, 0, 0)),
                  pl.BlockSpec((4, DIFF_DH), lambda h, b, i: (0, 0)),
                  pl.BlockSpec((DIFF_VD, 1), lambda h, b, i: (0, 0))],
        out_specs=pl.BlockSpec((1, t, LANE), lambda h, b, i: (b, i, h)),
        out_shape=jax.ShapeDtypeStruct((bsz, s, DIFF_HEADS * DIFF_VD), bf16),
        scratch_shapes=[pltpu.VMEM((LANE, nq), bf16),
                        pltpu.VMEM((s // t, DIFF_VD + ONES_ROWS, t), bf16)]
        + _SoftmaxState.scratch_shapes(t, nq, DIFF_VD),
        compiler_params=_cparams(3), name="diff_attention")(
            u, u, u, eq, ident, identq, wb, lam, sub_g)


def _stack_group_queries(q, e_ref, qT_s, tq):
    for g in range(4):
        qT_s[:, g * tq:(g + 1) * tq] = (lax.dot_general(
            e_ref[0, g], q, _NT, preferred_element_type=f32) * (WIN_DH ** -0.5 * LOG2E)).astype(bf16)


def _store_group_output(o, idq_ref, o_ref, tq):
    oT = jnp.concatenate([o[:, g * tq:(g + 1) * tq] for g in range(4)], axis=0).astype(bf16)
    o_ref[0] = lax.dot_general(idq_ref[...], oT, _NT, preferred_element_type=f32).astype(o_ref.dtype)


def _axial_kernel(q_ref, k_ref, v_ref, e_ref, ev_ref, idq_ref, o_ref,
                  qT_s, vT_s, *softmax_scratch, seq, tq, tk):
    qi = pl.program_id(2)
    n_chunks = seq // tk
    st = _SoftmaxState(*softmax_scratch)

    @pl.when(qi == 0)
    def _():
        _build_vT(v_ref, ev_ref[0], vT_s, n_chunks, tk)

    _stack_group_queries(q_ref[0], e_ref, qT_s, tq)

    def k_chunk(c):
        return k_ref[0, pl.ds(pl.multiple_of(c * tk, tk), tk), :]

    _attention_chunks(n_chunks, k_chunk, lambda c: None, qT_s, vT_s, st)
    _store_group_output(st.acc[0:AX_DH, :] / st.acc[AX_DH:AX_DH + 1, :], idq_ref, o_ref, tq)


def _axial_attention(q_rot, k_rot, u, e, ev, identq, tq, tk):
    bsz, s, _ = u.shape
    nq = 4 * tq
    kern = functools.partial(_axial_kernel, seq=s, tq=tq, tk=tk)
    return pl.pallas_call(
        kern, grid=(AX_KV, bsz, s // tq),
        in_specs=[pl.BlockSpec((1, tq, 2 * LANE), lambda h, b, i: (b, i, h)),
                  pl.BlockSpec((1, s, LANE), lambda h, b, i: (b, 0, 0)),
                  pl.BlockSpec((1, s, LANE), lambda h, b, i: (b, 0, _COL_DV)),
                  pl.BlockSpec((1, 4, LANE, 2 * LANE), lambda h, b, i: (h, 0, 0, 0)),
                  pl.BlockSpec((1, AX_DH, LANE), lambda h, b, i: (h, 0, 0)),
                  pl.BlockSpec((tq, tq), lambda h, b, i: (0, 0))],
        out_specs=pl.BlockSpec((1, tq, 2 * LANE), lambda h, b, i: (b, i, h)),
        out_shape=jax.ShapeDtypeStruct((bsz, s, AX_HEADS * AX_DH), bf16),
        scratch_shapes=[pltpu.VMEM((LANE, nq), bf16),
                        pltpu.VMEM((s // tk, AX_DH + ONES_ROWS, tk), bf16)]
        + _SoftmaxState.scratch_shapes(tk, nq, AX_DH),
        compiler_params=_cparams(3), name="axial_attention")(q_rot, k_rot, u, e, ev, identq)


def _rope_kernel(qa_ref, qb_ref, k_ref, cos_ref, sin_ref, qg_ref, kg_ref, qo_ref, ko_ref):
    lane = lax.broadcasted_iota(jnp.int32, (1, LANE), 1)
    low_head = lane < AX_DH
    first_half = (lane % (AX_DH // 2)) < (AX_DH // 4)
    cos = cos_ref[...]
    sin = sin_ref[...]

    def norm_rope(x, g):
        x = x.astype(f32)
        sq = x * x
        ms_lo = jnp.sum(jnp.where(low_head, sq, 0.0), axis=1, keepdims=True) * (1.0 / AX_DH)
        ms_hi = jnp.sum(jnp.where(low_head, 0.0, sq), axis=1, keepdims=True) * (1.0 / AX_DH)
        r = jnp.where(low_head, lax.rsqrt(ms_lo + LN_EPS), lax.rsqrt(ms_hi + LN_EPS))
        xn = x * r * g
        partner = jnp.where(first_half, pltpu.roll(xn, LANE - AX_DH // 4, axis=1),
                            pltpu.roll(xn, AX_DH // 4, axis=1))
        return xn * cos + partner * sin

    qg = qg_ref[...]
    qa = qa_ref[...]
    qb = qb_ref[...]
    for j in range(2):
        qo_ref[:, j * LANE:(j + 1) * LANE] = norm_rope(qa[:, j * LANE:(j + 1) * LANE], qg).astype(bf16)
        qo_ref[:, (2 + j) * LANE:(3 + j) * LANE] = norm_rope(qb[:, j * LANE:(j + 1) * LANE], qg).astype(bf16)
    ko_ref[...] = norm_rope(k_ref[...], kg_ref[...]).astype(bf16)


def _axial_rope(u2, cos_t, sin_t, qn_g, kn_g, seq, tm):
    t = u2.shape[0]
    n_pos = seq // tm
    return pl.pallas_call(
        _rope_kernel, grid=(t // tm,),
        in_specs=[pl.BlockSpec((tm, 2 * LANE), lambda i: (i, _COL_DQ // 2)),
                  pl.BlockSpec((tm, 2 * LANE), lambda i: (i, _COL_DQ // 2 + 1)),
                  pl.BlockSpec((tm, LANE), lambda i: (i, _COL_DK)),
                  pl.BlockSpec((tm, LANE), lambda i: (i % n_pos, 0)),
                  pl.BlockSpec((tm, LANE), lambda i: (i % n_pos, 0)),
                  pl.BlockSpec((1, LANE), lambda i: (0, 0)),
                  pl.BlockSpec((1, LANE), lambda i: (0, 0))],
        out_specs=[pl.BlockSpec((tm, AX_HEADS * AX_DH), lambda i: (i, 0)),
                   pl.BlockSpec((tm, LANE), lambda i: (i, 0))],
        out_shape=[jax.ShapeDtypeStruct((t, AX_HEADS * AX_DH), bf16),
                   jax.ShapeDtypeStruct((t, AX_KV * AX_DH), bf16)],
        compiler_params=_cparams(1), name="axial_rope")(u2, u2, u2, cos_t, sin_t, qn_g, kn_g)


def _window_kernel(q_ref, k_ref, v_ref, e_ref, ev_ref, idq_ref, bw_ref, sink_ref, o_ref,
                   qT_s, vT_s, s_s, p_s, *, seq, tq, span):
    qi = pl.program_id(2)

    @pl.when(qi == 0)
    def _():
        _build_vT(v_ref, ev_ref[0], vT_s, seq // LANE, LANE)

    _stack_group_queries(q_ref[0], e_ref, qT_s, tq)
    start = pl.multiple_of(jnp.clip(qi * tq - WIN, 0, seq - span), LANE)
    kw = k_ref[0, pl.ds(start, span), :]
    sink = sink_ref[0]
    m = jnp.maximum(_score_pass(kw, qT_s, s_s, bw_ref.at[0, 0]), sink)
    _exp2_pass(s_s, p_s, m)
    c0 = start // LANE
    acc = jnp.dot(vT_s[c0], p_s[0:LANE, :], preferred_element_type=f32)
    for j in range(1, span // LANE):
        acc = acc + jnp.dot(vT_s[c0 + j], p_s[j * LANE:(j + 1) * LANE, :], preferred_element_type=f32)
    l = acc[WIN_DH:WIN_DH + 1] + jnp.exp2(sink - m)
    _store_group_output(acc[0:WIN_DH] / l, idq_ref, o_ref, tq)


def _window_attention(u, e, ev, identq, bw, sink, tq, span):
    bsz, s, _ = u.shape
    nq = 4 * tq
    n_tiles = s // tq
    kern = functools.partial(_window_kernel, seq=s, tq=tq, span=span)
    variant = lambda i: jnp.where(i == 0, 0, jnp.where(i == n_tiles - 1, 2, 1))
    return pl.pallas_call(
        kern, grid=(WIN_KV, bsz, n_tiles),
        in_specs=[pl.BlockSpec((1, tq, 2 * LANE), lambda h, b, i: (b, i, _COL_CQ // 2 + h)),
                  pl.BlockSpec((1, s, LANE), lambda h, b, i: (b, 0, _COL_CK)),
                  pl.BlockSpec((1, s, LANE), lambda h, b, i: (b, 0, _COL_CV)),
                  pl.BlockSpec((1, 4, LANE, 2 * LANE), lambda h, b, i: (h, 0, 0, 0)),
                  pl.BlockSpec((1, WIN_DH, LANE), lambda h, b, i: (h, 0, 0)),
                  pl.BlockSpec((tq, tq), lambda h, b, i: (0, 0)),
                  pl.BlockSpec((1, 1, span, nq), lambda h, b, i: (variant(i), h, 0, 0)),
                  pl.BlockSpec((1, 1, nq), lambda h, b, i: (h, 0, 0))],
        out_specs=pl.BlockSpec((1, tq, 2 * LANE), lambda h, b, i: (b, i, h)),
        out_shape=jax.ShapeDtypeStruct((bsz, s, WIN_HEADS * WIN_DH), bf16),
        scratch_shapes=[pltpu.VMEM((LANE, nq), bf16),
                        pltpu.VMEM((s // LANE, WIN_DH + ONES_ROWS, LANE), bf16),
                        pltpu.VMEM((span, nq), f32),
                        pltpu.VMEM((span, nq), bf16)],
        compiler_params=_cparams(3), name="window_attention")(u, u, u, e, ev, identq, bw, sink)


def _merge_kernel(oa_ref, ob_ref, oc_ref, od_ref, gl_ref, x_ref, wb_ref, wo_ref, g_ref, b_ref, o_ref):
    merged = None
    for n, o_n in enumerate((oa_ref, ob_ref, oc_ref, od_ref)):
        gate = jax.nn.sigmoid(gl_ref[:, n * D_MODEL:(n + 1) * D_MODEL].astype(f32))
        term = gate * jnp.dot(o_n[...], wb_ref[n], preferred_element_type=f32)
        merged = term if merged is None else merged + term
    h = jnp.dot(merged.astype(bf16), wo_ref[...], preferred_element_type=f32)
    o_ref[...] = _layer_norm(DN_ALPHA * x_ref[...] + h, g_ref[...], b_ref[...])


def _merge(o_a, o_b, o_c, o_d, u2, x2, w_branch, w_out, ln_g, ln_b, tm):
    t = x2.shape[0]
    row = lambda i: (i, 0)
    full2 = lambda i: (0, 0)
    return pl.pallas_call(
        _merge_kernel, grid=(t // tm,),
        in_specs=[pl.BlockSpec((tm, BRANCH_W), row)] * 4 + [
            pl.BlockSpec((tm, N_BRANCH * D_MODEL), lambda i: (i, 1)),
            pl.BlockSpec((tm, D_MODEL), row),
            pl.BlockSpec((N_BRANCH, BRANCH_W, D_MODEL), lambda i: (0, 0, 0)),
            pl.BlockSpec((D_MODEL, D_MODEL), full2),
            pl.BlockSpec((1, D_MODEL), full2),
            pl.BlockSpec((1, D_MODEL), full2)],
        out_specs=pl.BlockSpec((tm, D_MODEL), row),
        out_shape=jax.ShapeDtypeStruct((t, D_MODEL), f32),
        compiler_params=_cparams(1), name="merge")(o_a, o_b, o_c, o_d, u2, x2, w_branch, w_out, ln_g, ln_b)


def _cross_kernel(x_ref, kv_ref, wq_ref, wo_ref, g_ref, b_ref, o_ref, oc_s):
    x = x_ref[0]
    q = jnp.dot(x.astype(bf16), wq_ref[...], preferred_element_type=f32) * (X_DH ** -0.5)
    q = q.astype(bf16)
    for h in range(X_HEADS):
        kh = kv_ref[0, :, h * X_DH:(h + 1) * X_DH]
        vh = kv_ref[0, :, D_MODEL + h * X_DH:D_MODEL + (h + 1) * X_DH]
        s = lax.dot_general(q[:, h * X_DH:(h + 1) * X_DH], kh, _NT, preferred_element_type=f32)
        m = jnp.max(s, axis=1, keepdims=True)
        p = jnp.exp(s - m)
        l = jnp.sum(p, axis=1, keepdims=True)
        oh = jnp.dot(p.astype(bf16), vh, preferred_element_type=f32) / l
        oc_s[:, h * X_DH:(h + 1) * X_DH] = oh.astype(bf16)
    hcross = jnp.dot(oc_s[...], wo_ref[...], preferred_element_type=f32)
    o_ref[0] = _layer_norm(DN_ALPHA * x + hcross, g_ref[...], b_ref[...])


def _cross_attention(x, kv, w_q, w_o, ln_g, ln_b, tm):
    bsz, s, _ = x.shape
    mlen = kv.shape[1]
    full2 = lambda b, i: (0, 0)
    return pl.pallas_call(
        _cross_kernel, grid=(bsz, s // tm),
        in_specs=[pl.BlockSpec((1, tm, D_MODEL), lambda b, i: (b, i, 0)),
                  pl.BlockSpec((1, mlen, 2 * D_MODEL), lambda b, i: (b, 0, 0)),
                  pl.BlockSpec((D_MODEL, D_MODEL), full2),
                  pl.BlockSpec((D_MODEL, D_MODEL), full2),
                  pl.BlockSpec((1, D_MODEL), full2),
                  pl.BlockSpec((1, D_MODEL), full2)],
        out_specs=pl.BlockSpec((1, tm, D_MODEL), lambda b, i: (b, i, 0)),
        out_shape=jax.ShapeDtypeStruct((bsz, s, D_MODEL), f32),
        scratch_shapes=[pltpu.VMEM((tm, D_MODEL), bf16)],
        compiler_params=_cparams(2), name="cross_attention")(x, kv, w_q, w_o, ln_g, ln_b)


def _ffn_kernel(prev_ref, cur_ref, next_ref, wg_ref, wu_ref, cwg_ref, cwu_ref, cbg_ref, cbu_ref,
                wd_ref, g_ref, b_ref, o_ref, a_s, acc_s, *, tm, halo, sub):
    i = pl.program_id(1)
    f = pl.program_id(2)
    last = pl.num_programs(1) - 1
    x = cur_ref[0]
    xp = jnp.where(i == 0, 0.0, prev_ref[0])
    xn = jnp.where(i == last, 0.0, next_ref[0])
    xb = jnp.concatenate([xp, x, xn], axis=0).astype(bf16)
    rows = tm + 2 * halo

    def conv3(h, w, b):
        below = pltpu.roll(h, 1, axis=0)
        above = pltpu.roll(h, rows - 1, axis=0)
        y = below * w[0:1] + h * w[1:2] + above * w[2:3] + b
        return y[halo:halo + tm]

    for j in range(0, FFN_CHUNK, sub):
        wdt = min(sub, FFN_CHUNK - j)
        hg = jnp.dot(xb, wg_ref[:, j:j + wdt], preferred_element_type=f32)
        hu = jnp.dot(xb, wu_ref[:, j:j + wdt], preferred_element_type=f32)
        yg = conv3(hg, cwg_ref[:, j:j + wdt], cbg_ref[:, j:j + wdt])
        yu = conv3(hu, cwu_ref[:, j:j + wdt], cbu_ref[:, j:j + wdt])
        act = 0.5 * yg * (1.0 + lax.erf(yg * (2.0 ** -0.5)))
        a_s[:, j:j + wdt] = (act * yu).astype(bf16)
    part = jnp.dot(a_s[...], wd_ref[...], preferred_element_type=f32)

    @pl.when(f == 0)
    def _():
        acc_s[...] = part

    @pl.when(f == pl.num_programs(2) - 1)
    def _():
        o_ref[0] = _layer_norm(DN_ALPHA * x + acc_s[...] + part, g_ref[...], b_ref[...])


def _conv_ffn(x, wg, wu, cwg, cwu, cbg, cbu, wd, ln_g, ln_b, tm):
    bsz, s, _ = x.shape
    halo = 8
    nb = tm // halo
    n_f = FFN_PAD // FFN_CHUNK
    assert n_f == 2
    kern = functools.partial(_ffn_kernel, tm=tm, halo=halo, sub=2 * LANE)
    fcol = lambda b, i, f: (0, f)
    full2 = lambda b, i, f: (0, 0)
    return pl.pallas_call(
        kern, grid=(bsz, s // tm, n_f),
        in_specs=[pl.BlockSpec((1, halo, D_MODEL), lambda b, i, f: (b, jnp.maximum(i * nb - 1, 0), 0)),
                  pl.BlockSpec((1, tm, D_MODEL), lambda b, i, f: (b, i, 0)),
                  pl.BlockSpec((1, halo, D_MODEL),
                               lambda b, i, f: (b, jnp.minimum((i + 1) * nb, s // halo - 1), 0)),
                  pl.BlockSpec((D_MODEL, FFN_CHUNK), fcol),
                  pl.BlockSpec((D_MODEL, FFN_CHUNK), fcol),
                  pl.BlockSpec((FFN_CONV_K, FFN_CHUNK), fcol),
                  pl.BlockSpec((FFN_CONV_K, FFN_CHUNK), fcol),
                  pl.BlockSpec((1, FFN_CHUNK), fcol),
                  pl.BlockSpec((1, FFN_CHUNK), fcol),
                  pl.BlockSpec((FFN_CHUNK, D_MODEL), lambda b, i, f: (f, 0)),
                  pl.BlockSpec((1, D_MODEL), full2),
                  pl.BlockSpec((1, D_MODEL), full2)],
        out_specs=pl.BlockSpec((1, tm, D_MODEL), lambda b, i, f: (b, i, 0)),
        out_shape=jax.ShapeDtypeStruct((bsz, s, D_MODEL), f32),
        scratch_shapes=[pltpu.VMEM((tm, FFN_CHUNK), bf16),
                        pltpu.VMEM((tm, D_MODEL), f32)],
        compiler_params=_cparams(3), name="conv_ffn")(
            x, x, x, wg, wu, cwg, cwu, cbg, cbu, wd, ln_g, ln_b)


def _t5_bucket(rel):
    half = NUM_BUCKETS // 2
    max_exact = half // 2
    n = jnp.abs(rel)
    nf = jnp.maximum(n, 1).astype(jnp.float32)
    large = max_exact + (jnp.log(nf / max_exact) / math.log(MAX_DISTANCE / max_exact)
                         * (half - max_exact)).astype(jnp.int32)
    large = jnp.minimum(large, half - 1)
    return jnp.where(rel > 0, half, 0) + jnp.where(n < max_exact, n, large)


def _toeplitz(value_of, start, n_keys, n_queries):
    period = n_keys + n_queries
    w = jnp.arange(period)
    vec = value_of(start - jnp.where(w < n_queries, w, w - period))
    flat = jnp.tile(vec, (1,) * (vec.ndim - 1) + (n_keys,))[..., :n_keys * (period - 1)]
    return flat.reshape(vec.shape[:-1] + (n_keys, period - 1))[..., :n_queries]


def _diff_bias_tiles(rel_bias, t):
    table = rel_bias[:, :2 * DIFF_HEADS].reshape(NUM_BUCKETS, DIFF_HEADS, 2).astype(f32) * LOG2E
    tiles = []
    for v in range(-2, 3):
        bias = _toeplitz(lambda rel: jnp.moveaxis(table[_t5_bucket(rel)], 0, -1), v * t, t, t)
        tiles.append(jnp.transpose(bias, (0, 2, 1, 3)).reshape(DIFF_HEADS, t, 2 * t))
    return jnp.stack(tiles, axis=1)


def _window_bias_tiles(rel_bias, tq, span):
    table = rel_bias[:, 2 * DIFF_HEADS:].reshape(NUM_BUCKETS, WIN_KV, WIN_HEADS // WIN_KV).astype(f32) * LOG2E

    def masked_bias(rel):
        vals = jnp.where((jnp.abs(rel) <= WIN)[:, None, None], table[_t5_bucket(rel)], NEG)
        return jnp.moveaxis(vals, 0, -1)

    tiles = []
    for start in (0, -WIN, tq - span):
        bias = _toeplitz(masked_bias, start, span, tq)
        tiles.append(jnp.transpose(bias, (0, 2, 1, 3)).reshape(WIN_KV, span, 4 * tq))
    return jnp.stack(tiles, axis=0)


def _rope_tables(seq):
    rows = seq // GRID_W
    row = jnp.repeat(jnp.arange(rows, dtype=f32), GRID_W)
    col = jnp.tile(jnp.arange(GRID_W, dtype=f32), rows)
    n_freq = AX_DH // 4
    inv = ROPE_THETA ** (-jnp.arange(n_freq, dtype=f32) / n_freq)
    ang_r = row[:, None] * inv
    ang_c = col[:, None] * inv
    cos_h = jnp.concatenate([jnp.cos(ang_r), jnp.cos(ang_r), jnp.cos(ang_c), jnp.cos(ang_c)], -1)
    sin_h = jnp.concatenate([-jnp.sin(ang_r), jnp.sin(ang_r), -jnp.sin(ang_c), jnp.sin(ang_c)], -1)
    return jnp.tile(cos_h, (1, 2)), jnp.tile(sin_h, (1, 2))


def _selectors():
    r = jnp.arange(LANE)[:, None]
    c = jnp.arange(LANE)[None, :]
    eq = jnp.stack([jnp.where((r == c) & (r // DIFF_DH == m), 1.0, 0.0) for m in range(2)])
    c2 = jnp.arange(2 * LANE)[None, :]
    e = jnp.stack([jnp.stack([jnp.where((r // WIN_DH == kv) & (c2 // WIN_DH == g)
                                        & (r % WIN_DH == c2 % WIN_DH), 1.0, 0.0)
                              for g in range(4)]) for kv in range(2)])
    d = jnp.arange(WIN_DH)[:, None]
    ev = jnp.stack([jnp.where(c == kv * WIN_DH + d, 1.0, 0.0) for kv in range(2)])
    return eq.astype(bf16), e.astype(bf16), ev.astype(bf16)


def _encode(x, mem, p, tables):
    bsz, s, _ = x.shape
    t_rows = bsz * s
    eq, e, ev, wb, bw = tables
    ident = jnp.eye(LANE, dtype=bf16)
    ident256 = jnp.eye(2 * LANE, dtype=bf16)
    ident512 = jnp.eye(4 * LANE, dtype=bf16)
    cos_t, sin_t = _rope_tables(s)
    for l in range(DEPTH):
        lam_init = 0.8 - 0.6 * math.exp(-0.3 * l)
        x2 = x.reshape(t_rows, D_MODEL)
        u2 = _project(x2, p['w_in'][l], p['b_in'][l], 1024, 1024, "in_proj")
        u = u2.reshape(bsz, s, N_IN)
        o_a = _conv_branch(u, p['a_conv_w'][l], p['a_conv_b'][l], p['a_ln_g'][l], p['a_ln_b'][l], 512)
        o_b = _diff_attention(u, eq, ident, ident512, wb, p['diff_lam'][l], p['diff_sub_g'][l],
                              lam_init, 512)
        o_c = _window_attention(u, e, ev, ident256, bw, p['win_sink'][l], 256, 512)
        q_rot, k_rot = _axial_rope(u2, cos_t, sin_t, p['ax_qn_g'][l], p['ax_kn_g'][l], s, 1024)
        o_d = _axial_attention(q_rot.reshape(bsz, s, -1), k_rot.reshape(bsz, s, -1), u, e, ev,
                               ident256, 256, 512)
        x2 = _merge(o_a.reshape(t_rows, -1), o_b.reshape(t_rows, -1), o_c.reshape(t_rows, -1),
                    o_d.reshape(t_rows, -1), u2, x2, p['w_branch'][l], p['w_mix_out'][l],
                    p['ln1_g'][l], p['ln1_b'][l], 512)
        kv = _project(mem.reshape(-1, D_MODEL), p['w_xkv'][l], None, 1024, 1024, "mem_proj")
        x = _cross_attention(x2.reshape(bsz, s, D_MODEL), kv.reshape(bsz, -1, 2 * D_MODEL),
                             p['w_xq'][l], p['w_xo'][l], p['ln2_g'][l], p['ln2_b'][l], 512)
        x = _conv_ffn(x, p['wg'][l], p['wu'][l], p['cwg'][l], p['cwu'][l], p['cbg'][l], p['cbu'][l],
                      p['wd'][l], p['ln3_g'][l], p['ln3_b'][l], 512)
    return x


def _pad_last(a, n):
    return jnp.pad(a, [(0, 0)] * (a.ndim - 1) + [(0, n - a.shape[-1])])


def kernel(x_prompt, x_sample, mem_prompt, mem_sample, rel_bias, w_in, b_in, a_conv_w, a_conv_b, a_ln_g,
           a_ln_b, diff_lam, diff_sub_g, win_sink, ax_qn_g, ax_kn_g, w_branch, w_mix_out, ln1_g, ln1_b,
           w_xq, w_xkv, w_xo, ln2_g, ln2_b, w_up, f_conv_w, f_conv_b, w_down, ln3_g, ln3_b):
    depth = w_in.shape[0]
    row = lambda a: a.reshape(depth, 1, a.shape[-1])
    tq_win = 256
    p = {
        'w_in': w_in.astype(bf16), 'b_in': row(b_in),
        'a_conv_w': a_conv_w, 'a_conv_b': row(a_conv_b), 'a_ln_g': row(a_ln_g), 'a_ln_b': row(a_ln_b),
        'diff_lam': diff_lam, 'diff_sub_g': diff_sub_g.reshape(depth, DIFF_VD, 1),
        'win_sink': jnp.repeat(win_sink.reshape(depth, WIN_KV, 1, WIN_HEADS // WIN_KV) * LOG2E, tq_win, axis=-1),
        'ax_qn_g': jnp.tile(row(ax_qn_g), (1, 1, 2)), 'ax_kn_g': jnp.tile(row(ax_kn_g), (1, 1, 2)),
        'w_branch': w_branch.astype(bf16), 'w_mix_out': w_mix_out.astype(bf16),
        'ln1_g': row(ln1_g), 'ln1_b': row(ln1_b),
        'w_xq': w_xq.astype(bf16), 'w_xkv': w_xkv.astype(bf16), 'w_xo': w_xo.astype(bf16),
        'ln2_g': row(ln2_g), 'ln2_b': row(ln2_b),
        'wg': _pad_last(w_up[..., :FFN_DIM], FFN_PAD).astype(bf16),
        'wu': _pad_last(w_up[..., FFN_DIM:], FFN_PAD).astype(bf16),
        'cwg': _pad_last(f_conv_w[..., :FFN_DIM], FFN_PAD), 'cwu': _pad_last(f_conv_w[..., FFN_DIM:], FFN_PAD),
        'cbg': _pad_last(row(f_conv_b[..., :FFN_DIM]), FFN_PAD),
        'cbu': _pad_last(row(f_conv_b[..., FFN_DIM:]), FFN_PAD),
        'wd': jnp.pad(w_down, ((0, 0), (0, FFN_PAD - FFN_DIM), (0, 0))).astype(bf16),
        'ln3_g': row(ln3_g), 'ln3_b': row(ln3_b),
    }
    eq, e, ev = _selectors()
    wb = _diff_bias_tiles(rel_bias, 512)
    bw = _window_bias_tiles(rel_bias, tq_win, 512)
    tables = (eq, e, ev, wb, bw)
    return (_encode(x_prompt, mem_prompt, p, tables), _encode(x_sample, mem_sample, p, tables))
```

```python
import functools
import math

import jax
import jax.numpy as jnp
from jax import lax
from jax.experimental import pallas as pl
from jax.experimental.pallas import tpu as pltpu

f32 = jnp.float32
bf16 = jnp.bfloat16

D_MODEL = 1024
DEPTH = 2
GRID_W = 64
N_BRANCH = 4
BRANCH_W = 512
CONV_K = 31
DIFF_HEADS = 4
DIFF_DH = 64
DIFF_VD = 2 * DIFF_DH
WIN = 128
WIN_HEADS = 8
WIN_KV = 2
WIN_DH = 64
AX_HEADS = 8
AX_KV = 2
AX_DH = 64
ROPE_THETA = 10000.0
X_HEADS = 4
X_DH = D_MODEL // X_HEADS
FFN_DIM = 2752
FFN_CONV_K = 3
NUM_BUCKETS = 32
MAX_DISTANCE = 128
LN_EPS = 1e-5
NEG = -1e30
DN_ALPHA = (2 * DEPTH) ** 0.25
LOG2E = math.log2(math.e)
N_IN = 8192

_COL_UA = 0
_COL_BQ = 8
_COL_BK = 12
_COL_BV = 16
_COL_CQ = 20
_COL_CK = 24
_COL_CV = 25
_COL_DQ = 26
_COL_DK = 30
_COL_DV = 31
_COL_GL = 32

LANE = 128
FFN_PAD = 2816
VMEM_LIMIT = 48 * 1024 * 1024

_NT = (((1,), (1,)), ((), ()))


def _cparams(n_axes):
    return pltpu.CompilerParams(dimension_semantics=("arbitrary",) * n_axes,
                                vmem_limit_bytes=VMEM_LIMIT)


def _layer_norm(y, g, b):
    mu = jnp.mean(y, axis=-1, keepdims=True)
    yc = y - mu
    var = jnp.mean(yc * yc, axis=-1, keepdims=True)
    return yc * lax.rsqrt(var + LN_EPS) * g + b


def _proj_bias_kernel(x_ref, w_ref, b_ref, o_ref):
    acc = jnp.dot(x_ref[...].astype(bf16), w_ref[...], preferred_element_type=f32)
    o_ref[...] = (acc + b_ref[...]).astype(o_ref.dtype)


def _proj_kernel(x_ref, w_ref, o_ref):
    acc = jnp.dot(x_ref[...].astype(bf16), w_ref[...], preferred_element_type=f32)
    o_ref[...] = acc.astype(o_ref.dtype)


def _project(x, w, b, tm, tn, name):
    m, k = x.shape
    n = w.shape[1]
    tm = min(tm, m)
    in_specs = [pl.BlockSpec((tm, k), lambda i, j: (i, 0)),
                pl.BlockSpec((k, tn), lambda i, j: (0, j))]
    args = [x, w]
    body = _proj_kernel
    if b is not None:
        in_specs.append(pl.BlockSpec((1, tn), lambda i, j: (0, j)))
        args.append(b)
        body = _proj_bias_kernel
    return pl.pallas_call(
        body, grid=(m // tm, n // tn), in_specs=in_specs,
        out_specs=pl.BlockSpec((tm, tn), lambda i, j: (i, j)),
        out_shape=jax.ShapeDtypeStruct((m, n), bf16),
        compiler_params=_cparams(2), name=name)(*args)


SUBLANES = 8
CONV_ROWS = 32


def _conv_kernel(prev_ref, cur_ref, next_ref, w_ref, cb_ref, g_ref, b_ref, o_ref, h_s, y_s, *, tm, halo):
    i = pl.program_id(1)
    last = pl.num_programs(1) - 1

    def glu(u):
        u = u.astype(f32)
        return u[:, :BRANCH_W] * jax.nn.sigmoid(u[:, BRANCH_W:])

    span = tm + 2 * halo - SUBLANES
    h_s[0, 0:halo, :] = jnp.where(i == 0, 0.0, glu(prev_ref[0]))
    h_s[0, halo:halo + tm, :] = glu(cur_ref[0])
    h_s[0, halo + tm:, :] = jnp.where(i == last, 0.0, glu(next_ref[0]))
    for b in range(1, SUBLANES):
        h_s[b, 0:span, :] = h_s[0, b:b + span, :]
    base = halo - CONV_K // 2

    def rows(r, carry):
        r0 = pl.multiple_of(r * CONV_ROWS, CONV_ROWS)
        acc = jnp.broadcast_to(cb_ref[...], (CONV_ROWS, BRANCH_W))
        for k in range(CONV_K):
            shift = (base + k) % SUBLANES
            acc = acc + h_s[shift, pl.ds(r0 + (base + k - shift), CONV_ROWS), :] * w_ref[k:k + 1, :]
        y_s[pl.ds(r0, CONV_ROWS), :] = acc
        return carry
    lax.fori_loop(0, tm // CONV_ROWS, rows, 0)
    y = _layer_norm(y_s[...], g_ref[...], b_ref[...])
    o_ref[0] = (y * jax.nn.sigmoid(y)).astype(o_ref.dtype)


def _conv_branch(u, conv_w, conv_b, ln_g, ln_b, tm):
    bsz, s, _ = u.shape
    halo = 16
    nb = tm // halo
    kern = functools.partial(_conv_kernel, tm=tm, halo=halo)
    full = lambda b, i: (0, 0)
    return pl.pallas_call(
        kern, grid=(bsz, s // tm),
        in_specs=[pl.BlockSpec((1, halo, 2 * BRANCH_W), lambda b, i: (b, jnp.maximum(i * nb - 1, 0), 0)),
                  pl.BlockSpec((1, tm, 2 * BRANCH_W), lambda b, i: (b, i, 0)),
                  pl.BlockSpec((1, halo, 2 * BRANCH_W),
                               lambda b, i: (b, jnp.minimum((i + 1) * nb, s // halo - 1), 0)),
                  pl.BlockSpec((CONV_K, BRANCH_W), full),
                  pl.BlockSpec((1, BRANCH_W), full),
                  pl.BlockSpec((1, BRANCH_W), full),
                  pl.BlockSpec((1, BRANCH_W), full)],
        out_specs=pl.BlockSpec((1, tm, BRANCH_W), lambda b, i: (b, i, 0)),
        out_shape=jax.ShapeDtypeStruct((bsz, s, BRANCH_W), bf16),
        scratch_shapes=[pltpu.VMEM((SUBLANES, tm + 2 * halo, BRANCH_W), f32),
                        pltpu.VMEM((tm, BRANCH_W), f32)],
        compiler_params=_cparams(2), name="conv_branch")(u, u, u, conv_w, conv_b, ln_g, ln_b)


COL_BLOCK = 2 * LANE


def _score_pass(kc, qT_s, s_s, bias_ref=None):
    nq = qT_s.shape[1]
    maxes = []
    for j in range(0, nq, COL_BLOCK):
        s = jnp.dot(kc, qT_s[:, j:j + COL_BLOCK], preferred_element_type=f32)
        if bias_ref is not None:
            s = s + bias_ref[:, j:j + COL_BLOCK]
        s_s[:, j:j + COL_BLOCK] = s
        maxes.append(jnp.max(s, axis=0, keepdims=True))
    return jnp.concatenate(maxes, axis=1)


def _exp2_pass(s_s, p_s, shift):
    tk, nq = s_s.shape
    shift16 = jnp.broadcast_to(shift, (16, nq))
    for r in range(0, tk, 16):
        p_s[r:r + 16, :] = jnp.exp2(s_s[r:r + 16, :] - shift16).astype(bf16)


ONES_ROWS = 16


class _SoftmaxState:
    def __init__(self, s0, s1, p0, p1, mc, m, alpha, acc):
        self.s = (s0, s1)
        self.p = (p0, p1)
        self.mc, self.m, self.alpha, self.acc = mc, m, alpha, acc

    @staticmethod
    def scratch_shapes(tk, nq, dv):
        return [pltpu.VMEM((tk, nq), f32), pltpu.VMEM((tk, nq), f32),
                pltpu.VMEM((tk, nq), bf16), pltpu.VMEM((tk, nq), bf16),
                pltpu.VMEM((1, nq), f32), pltpu.VMEM((1, nq), f32), pltpu.VMEM((1, nq), f32),
                pltpu.VMEM((dv + ONES_ROWS, nq), f32)]


def _attention_tiles(n_tiles, n_chunks, build_qT, k_chunk, bias_tile, finalize, qT_s, vT_s, st):
    st.acc[...] = jnp.ones(st.acc.shape, f32)
    st.alpha[...] = jnp.ones(st.alpha.shape, f32)
    st.p[1][...] = jnp.zeros(st.p[1].shape, bf16)

    def score(tile, c, buf, slot):
        st.mc[...] = _score_pass(k_chunk(c), qT_s.at[slot], st.s[buf], bias_tile(tile, c))

    def softmax(buf, first):
        m_old = jnp.full(st.m.shape, NEG, f32) if first else st.m[...]
        m_new = jnp.maximum(m_old, st.mc[...])
        _exp2_pass(st.s[buf], st.p[buf], m_new)
        st.alpha[...] = jnp.exp2(m_old - m_new)
        st.m[...] = m_new

    def pv(c, buf):
        st.acc[...] = st.alpha[...] * st.acc[...] + jnp.dot(vT_s[c], st.p[buf][...],
                                                            preferred_element_type=f32)

    build_qT(0, 0)
    score(0, 0, 0, 0)

    def tile_body(tile, carry):
        slot = tile % 2
        nxt = jnp.minimum(tile + 1, n_tiles - 1)
        build_qT(nxt, 1 - slot)
        pv(n_chunks - 1, 1)
        finalize(jnp.maximum(tile - 1, 0))
        softmax(0, first=True)
        score(tile, 1, 1, slot)

        def pair(j, carry):
            c = 2 * j + 1
            pv(c - 1, 0)
            softmax(1, first=False)
            score(tile, c + 1, 0, slot)
            pv(c, 1)
            softmax(0, first=False)
            score(tile, c + 2, 1, slot)
            return carry
        lax.fori_loop(0, (n_chunks - 2) // 2, pair, 0)
        pv(n_chunks - 2, 0)
        softmax(1, first=False)
        score(nxt, 0, 0, 1 - slot)
        return carry
    lax.fori_loop(0, n_tiles, tile_body, 0)
    pv(n_chunks - 1, 1)
    finalize(n_tiles - 1)


def _build_vT(v_ref, sel, vT_s, n_chunks, tk):
    dv = sel.shape[0]

    def body(c, carry):
        vc = v_ref[0, pl.ds(pl.multiple_of(c * tk, tk), tk), :]
        vT_s[c, 0:dv, :] = lax.dot_general(sel, vc, _NT, preferred_element_type=f32).astype(bf16)
        vT_s[c, dv:, :] = jnp.ones((ONES_ROWS, tk), bf16)
        return carry
    lax.fori_loop(0, n_chunks, body, 0)


def _diff_kernel(q_ref, k_ref, v_ref, eq_ref, id_ref, idq_ref, wb_ref, lam_ref, g_ref, o_ref,
                 qT_s, vT_s, *softmax_scratch, seq, t, lam_init):
    n = seq // t
    st = _SoftmaxState(*softmax_scratch)
    _build_vT(v_ref, id_ref[...], vT_s, n, t)

    def build_qT(tile, slot):
        q = q_ref[0, pl.ds(pl.multiple_of(tile * t, t), t), :]
        for m in range(2):
            qT_s[slot, :, m * t:(m + 1) * t] = (lax.dot_general(
                eq_ref[m], q, _NT, preferred_element_type=f32) * (DIFF_DH ** -0.5 * LOG2E)).astype(bf16)

    def k_chunk(c):
        return k_ref[0, pl.ds(pl.multiple_of(c * t, t), t), :]

    def bias_tile(tile, c):
        return wb_ref.at[0, jnp.clip(c - tile, -2, 2) + 2]

    lam = lam_ref[...]
    lmb = (jnp.exp(jnp.sum(lam[0:1] * lam[1:2], axis=1, keepdims=True))
           - jnp.exp(jnp.sum(lam[2:3] * lam[3:4], axis=1, keepdims=True)) + lam_init)

    def finalize(tile):
        o_maps = st.acc[0:DIFF_VD, :] / st.acc[DIFF_VD:DIFF_VD + 1, :]
        o = o_maps[:, :t] - lmb * o_maps[:, t:]
        ms = jnp.mean(o * o, axis=0, keepdims=True)
        y = o * lax.rsqrt(ms + LN_EPS) * g_ref[...] * (1.0 - lam_init)
        o_ref[0, pl.ds(pl.multiple_of(tile * t, t), t), :] = lax.dot_general(
            idq_ref[...], y.astype(bf16), _NT, preferred_element_type=f32).astype(o_ref.dtype)

    _attention_tiles(n, n, build_qT, k_chunk, bias_tile, finalize, qT_s, vT_s, st)


def _diff_attention(u, eq, ident, identq, wb, lam, sub_g, lam_init, t):
    bsz, s, _ = u.shape
    nq = 2 * t
    kern = functools.partial(_diff_kernel, seq=s, t=t, lam_init=lam_init)
    return pl.pallas_call(
        kern, grid=(DIFF_HEADS, bsz),
        in_specs=[pl.BlockSpec((1, s, LANE), lambda h, b: (b, 0, _COL_BQ + h)),
                  pl.BlockSpec((1, s, LANE), lambda h, b: (b, 0, _COL_BK + h)),
                  pl.BlockSpec((1, s, LANE), lambda h, b: (b, 0, _COL_BV + h)),
                  pl.BlockSpec((2, LANE, LANE), lambda h, b: (0, 0, 0)),
                  pl.BlockSpec((LANE, LANE), lambda h, b: (0, 0)),
                  pl.BlockSpec((t, t), lambda h, b: (0, 0)),
                  pl.BlockSpec((1, 5, t, nq), lambda h, b: (h, 0, 0, 0)),
                  pl.BlockSpec((4, DIFF_DH), lambda h, b: (0, 0)),
                  pl.BlockSpec((DIFF_VD, 1), lambda h, b: (0, 0))],
        out_specs=pl.BlockSpec((1, s, LANE), lambda h, b: (b, 0, h)),
        out_shape=jax.ShapeDtypeStruct((bsz, s, DIFF_HEADS * DIFF_VD), bf16),
        scratch_shapes=[pltpu.VMEM((2, LANE, nq), bf16),
                        pltpu.VMEM((s // t, DIFF_VD + ONES_ROWS, t), bf16)]
        + _SoftmaxState.scratch_shapes(t, nq, DIFF_VD),
        compiler_params=_cparams(2), name="diff_attention")(
            u, u, u, eq, ident, identq, wb, lam, sub_g)


def _stack_group_queries(q, e_ref, qT_ref, tq):
    for g in range(4):
        qT_ref[:, g * tq:(g + 1) * tq] = (lax.dot_general(
            e_ref[0, g], q, _NT, preferred_element_type=f32) * (WIN_DH ** -0.5 * LOG2E)).astype(bf16)


def _store_group_output(o, idq_ref, o_ref, row0, tq):
    oT = jnp.concatenate([o[:, g * tq:(g + 1) * tq] for g in range(4)], axis=0).astype(bf16)
    o_ref[0, pl.ds(row0, tq), :] = lax.dot_general(
        idq_ref[...], oT, _NT, preferred_element_type=f32).astype(o_ref.dtype)


def _axial_kernel(q_ref, k_ref, v_ref, e_ref, ev_ref, idq_ref, o_ref,
                  qT_s, vT_s, *softmax_scratch, seq, tq, tk):
    n_chunks = seq // tk
    st = _SoftmaxState(*softmax_scratch)
    _build_vT(v_ref, ev_ref[0], vT_s, n_chunks, tk)

    def build_qT(tile, slot):
        q = q_ref[0, pl.ds(pl.multiple_of(tile * tq, tq), tq), :]
        _stack_group_queries(q, e_ref, qT_s.at[slot], tq)

    def k_chunk(c):
        return k_ref[0, pl.ds(pl.multiple_of(c * tk, tk), tk), :]

    def finalize(tile):
        _store_group_output(st.acc[0:AX_DH, :] / st.acc[AX_DH:AX_DH + 1, :], idq_ref, o_ref,
                            pl.multiple_of(tile * tq, tq), tq)

    _attention_tiles(seq // tq, n_chunks, build_qT, k_chunk, lambda tile, c: None, finalize,
                     qT_s, vT_s, st)


def _axial_attention(q_rot, k_rot, u, e, ev, identq, tq, tk):
    bsz, s, _ = u.shape
    nq = 4 * tq
    kern = functools.partial(_axial_kernel, seq=s, tq=tq, tk=tk)
    return pl.pallas_call(
        kern, grid=(AX_KV, bsz),
        in_specs=[pl.BlockSpec((1, s, 2 * LANE), lambda h, b: (b, 0, h)),
                  pl.BlockSpec((1, s, LANE), lambda h, b: (b, 0, 0)),
                  pl.BlockSpec((1, s, LANE), lambda h, b: (b, 0, _COL_DV)),
                  pl.BlockSpec((1, 4, LANE, 2 * LANE), lambda h, b: (h, 0, 0, 0)),
                  pl.BlockSpec((1, AX_DH, LANE), lambda h, b: (h, 0, 0)),
                  pl.BlockSpec((tq, tq), lambda h, b: (0, 0))],
        out_specs=pl.BlockSpec((1, s, 2 * LANE), lambda h, b: (b, 0, h)),
        out_shape=jax.ShapeDtypeStruct((bsz, s, AX_HEADS * AX_DH), bf16),
        scratch_shapes=[pltpu.VMEM((2, LANE, nq), bf16),
                        pltpu.VMEM((s // tk, AX_DH + ONES_ROWS, tk), bf16)]
        + _SoftmaxState.scratch_shapes(tk, nq, AX_DH),
        compiler_params=_cparams(2), name="axial_attention")(q_rot, k_rot, u, e, ev, identq)


def _rope_kernel(qa_ref, qb_ref, k_ref, cos_ref, sin_ref, qg_ref, kg_ref, qo_ref, ko_ref):
    lane = lax.broadcasted_iota(jnp.int32, (1, LANE), 1)
    low_head = lane < AX_DH
    first_half = (lane % (AX_DH // 2)) < (AX_DH // 4)
    cos = cos_ref[...]
    sin = sin_ref[...]

    def norm_rope(x, g):
        x = x.astype(f32)
        sq = x * x
        ms_lo = jnp.sum(jnp.where(low_head, sq, 0.0), axis=1, keepdims=True) * (1.0 / AX_DH)
        ms_hi = jnp.sum(jnp.where(low_head, 0.0, sq), axis=1, keepdims=True) * (1.0 / AX_DH)
        r = jnp.where(low_head, lax.rsqrt(ms_lo + LN_EPS), lax.rsqrt(ms_hi + LN_EPS))
        xn = x * r * g
        partner = jnp.where(first_half, pltpu.roll(xn, LANE - AX_DH // 4, axis=1),
                            pltpu.roll(xn, AX_DH // 4, axis=1))
        return xn * cos + partner * sin

    qg = qg_ref[...]
    qa = qa_ref[...]
    qb = qb_ref[...]
    for j in range(2):
        qo_ref[:, j * LANE:(j + 1) * LANE] = norm_rope(qa[:, j * LANE:(j + 1) * LANE], qg).astype(bf16)
        qo_ref[:, (2 + j) * LANE:(3 + j) * LANE] = norm_rope(qb[:, j * LANE:(j + 1) * LANE], qg).astype(bf16)
    ko_ref[...] = norm_rope(k_ref[...], kg_ref[...]).astype(bf16)


def _axial_rope(u2, cos_t, sin_t, qn_g, kn_g, seq, tm):
    t = u2.shape[0]
    n_pos = seq // tm
    return pl.pallas_call(
        _rope_kernel, grid=(t // tm,),
        in_specs=[pl.BlockSpec((tm, 2 * LANE), lambda i: (i, _COL_DQ // 2)),
                  pl.BlockSpec((tm, 2 * LANE), lambda i: (i, _COL_DQ // 2 + 1)),
                  pl.BlockSpec((tm, LANE), lambda i: (i, _COL_DK)),
                  pl.BlockSpec((tm, LANE), lambda i: (i % n_pos, 0)),
                  pl.BlockSpec((tm, LANE), lambda i: (i % n_pos, 0)),
                  pl.BlockSpec((1, LANE), lambda i: (0, 0)),
                  pl.BlockSpec((1, LANE), lambda i: (0, 0))],
        out_specs=[pl.BlockSpec((tm, AX_HEADS * AX_DH), lambda i: (i, 0)),
                   pl.BlockSpec((tm, LANE), lambda i: (i, 0))],
        out_shape=[jax.ShapeDtypeStruct((t, AX_HEADS * AX_DH), bf16),
                   jax.ShapeDtypeStruct((t, AX_KV * AX_DH), bf16)],
        compiler_params=_cparams(1), name="axial_rope")(u2, u2, u2, cos_t, sin_t, qn_g, kn_g)


def _window_kernel(q_ref, k_ref, v_ref, e_ref, ev_ref, idq_ref, bw_ref, sink_ref, o_ref,
                   qT_s, vT_s, s0, s1, p0, p1, m0, m1, *, seq, tq, span):
    n_tiles = seq // tq
    s_bufs, p_bufs, m_bufs = (s0, s1), (p0, p1), (m0, m1)
    _build_vT(v_ref, ev_ref[0], vT_s, seq // LANE, LANE)
    sink = sink_ref[0]

    def span_start(tile):
        return pl.multiple_of(jnp.clip(tile * tq - WIN, 0, seq - span), LANE)

    def score(tile, buf):
        q = q_ref[0, pl.ds(pl.multiple_of(tile * tq, tq), tq), :]
        _stack_group_queries(q, e_ref, qT_s, tq)
        kw = k_ref[0, pl.ds(span_start(tile), span), :]
        variant = jnp.where(tile == 0, 0, jnp.where(tile == n_tiles - 1, 2, 1))
        m_bufs[buf][...] = jnp.maximum(_score_pass(kw, qT_s, s_bufs[buf], bw_ref.at[variant, 0]), sink)

    def softmax(buf):
        _exp2_pass(s_bufs[buf], p_bufs[buf], m_bufs[buf][...])

    def pv(tile, buf):
        tail = jnp.exp2(sink - m_bufs[buf][...])
        c0 = span_start(tile) // LANE
        acc = jnp.dot(vT_s[c0], p_bufs[buf][0:LANE, :], preferred_element_type=f32)
        for j in range(1, span // LANE):
            acc = acc + jnp.dot(vT_s[c0 + j], p_bufs[buf][j * LANE:(j + 1) * LANE, :],
                                preferred_element_type=f32)
        l = acc[WIN_DH:WIN_DH + 1] + tail
        _store_group_output(acc[0:WIN_DH] / l, idq_ref, o_ref, pl.multiple_of(tile * tq, tq), tq)

    def stage(tile, buf):
        pv(tile - 1, 1 - buf)
        softmax(buf)
        score(tile + 1, 1 - buf)

    score(0, 0)
    softmax(0)
    score(1, 1)

    def pair(j, carry):
        stage(2 * j + 1, 1)
        stage(2 * j + 2, 0)
        return carry
    lax.fori_loop(0, (n_tiles - 2) // 2, pair, 0)
    pv(n_tiles - 2, 0)
    softmax(1)
    pv(n_tiles - 1, 1)


def _window_attention(u, e, ev, identq, bw, sink, tq, span):
    bsz, s, _ = u.shape
    nq = 4 * tq
    kern = functools.partial(_window_kernel, seq=s, tq=tq, span=span)
    return pl.pallas_call(
        kern, grid=(WIN_KV, bsz),
        in_specs=[pl.BlockSpec((1, s, 2 * LANE), lambda h, b: (b, 0, _COL_CQ // 2 + h)),
                  pl.BlockSpec((1, s, LANE), lambda h, b: (b, 0, _COL_CK)),
                  pl.BlockSpec((1, s, LANE), lambda h, b: (b, 0, _COL_CV)),
                  pl.BlockSpec((1, 4, LANE, 2 * LANE), lambda h, b: (h, 0, 0, 0)),
                  pl.BlockSpec((1, WIN_DH, LANE), lambda h, b: (h, 0, 0)),
                  pl.BlockSpec((tq, tq), lambda h, b: (0, 0)),
                  pl.BlockSpec((3, 1, span, nq), lambda h, b: (0, h, 0, 0)),
                  pl.BlockSpec((1, 1, nq), lambda h, b: (h, 0, 0))],
        out_specs=pl.BlockSpec((1, s, 2 * LANE), lambda h, b: (b, 0, h)),
        out_shape=jax.ShapeDtypeStruct((bsz, s, WIN_HEADS * WIN_DH), bf16),
        scratch_shapes=[pltpu.VMEM((LANE, nq), bf16),
                        pltpu.VMEM((s // LANE, WIN_DH + ONES_ROWS, LANE), bf16),
                        pltpu.VMEM((span, nq), f32), pltpu.VMEM((span, nq), f32),
                        pltpu.VMEM((span, nq), bf16), pltpu.VMEM((span, nq), bf16),
                        pltpu.VMEM((1, nq), f32), pltpu.VMEM((1, nq), f32)],
        compiler_params=_cparams(2), name="window_attention")(u, u, u, e, ev, identq, bw, sink)


def _merge_kernel(oa_ref, ob_ref, oc_ref, od_ref, gl_ref, x_ref, wb_ref, wo_ref, g_ref, b_ref, o_ref):
    merged = None
    for n, o_n in enumerate((oa_ref, ob_ref, oc_ref, od_ref)):
        gate = jax.nn.sigmoid(gl_ref[:, n * D_MODEL:(n + 1) * D_MODEL].astype(f32))
        term = gate * jnp.dot(o_n[...], wb_ref[n], preferred_element_type=f32)
        merged = term if merged is None else merged + term
    h = jnp.dot(merged.astype(bf16), wo_ref[...], preferred_element_type=f32)
    o_ref[...] = _layer_norm(DN_ALPHA * x_ref[...] + h, g_ref[...], b_ref[...])


def _merge(o_a, o_b, o_c, o_d, u2, x2, w_branch, w_out, ln_g, ln_b, tm):
    t = x2.shape[0]
    row = lambda i: (i, 0)
    full2 = lambda i: (0, 0)
    return pl.pallas_call(
        _merge_kernel, grid=(t // tm,),
        in_specs=[pl.BlockSpec((tm, BRANCH_W), row)] * 4 + [
            pl.BlockSpec((tm, N_BRANCH * D_MODEL), lambda i: (i, 1)),
            pl.BlockSpec((tm, D_MODEL), row),
            pl.BlockSpec((N_BRANCH, BRANCH_W, D_MODEL), lambda i: (0, 0, 0)),
            pl.BlockSpec((D_MODEL, D_MODEL), full2),
            pl.BlockSpec((1, D_MODEL), full2),
            pl.BlockSpec((1, D_MODEL), full2)],
        out_specs=pl.BlockSpec((tm, D_MODEL), row),
        out_shape=jax.ShapeDtypeStruct((t, D_MODEL), f32),
        compiler_params=_cparams(1), name="merge")(o_a, o_b, o_c, o_d, u2, x2, w_branch, w_out, ln_g, ln_b)


def _cross_kernel(x_ref, kv_ref, wq_ref, wo_ref, g_ref, b_ref, o_ref, oc_s):
    x = x_ref[0]
    q = jnp.dot(x.astype(bf16), wq_ref[...], preferred_element_type=f32) * (X_DH ** -0.5)
    q = q.astype(bf16)
    for h in range(X_HEADS):
        kh = kv_ref[0, :, h * X_DH:(h + 1) * X_DH]
        vh = kv_ref[0, :, D_MODEL + h * X_DH:D_MODEL + (h + 1) * X_DH]
        s = lax.dot_general(q[:, h * X_DH:(h + 1) * X_DH], kh, _NT, preferred_element_type=f32)
        m = jnp.max(s, axis=1, keepdims=True)
        p = jnp.exp(s - m)
        l = jnp.sum(p, axis=1, keepdims=True)
        oh = jnp.dot(p.astype(bf16), vh, preferred_element_type=f32) / l
        oc_s[:, h * X_DH:(h + 1) * X_DH] = oh.astype(bf16)
    hcross = jnp.dot(oc_s[...], wo_ref[...], preferred_element_type=f32)
    o_ref[0] = _layer_norm(DN_ALPHA * x + hcross, g_ref[...], b_ref[...])


def _cross_attention(x, kv, w_q, w_o, ln_g, ln_b, tm):
    bsz, s, _ = x.shape
    mlen = kv.shape[1]
    full2 = lambda b, i: (0, 0)
    return pl.pallas_call(
        _cross_kernel, grid=(bsz, s // tm),
        in_specs=[pl.BlockSpec((1, tm, D_MODEL), lambda b, i: (b, i, 0)),
                  pl.BlockSpec((1, mlen, 2 * D_MODEL), lambda b, i: (b, 0, 0)),
                  pl.BlockSpec((D_MODEL, D_MODEL), full2),
                  pl.BlockSpec((D_MODEL, D_MODEL), full2),
                  pl.BlockSpec((1, D_MODEL), full2),
                  pl.BlockSpec((1, D_MODEL), full2)],
        out_specs=pl.BlockSpec((1, tm, D_MODEL), lambda b, i: (b, i, 0)),
        out_shape=jax.ShapeDtypeStruct((bsz, s, D_MODEL), f32),
        scratch_shapes=[pltpu.VMEM((tm, D_MODEL), bf16)],
        compiler_params=_cparams(2), name="cross_attention")(x, kv, w_q, w_o, ln_g, ln_b)


def _ffn_kernel(prev_ref, cur_ref, next_ref, wg_ref, wu_ref, cwg_ref, cwu_ref, cbg_ref, cbu_ref,
                wd_ref, g_ref, b_ref, o_ref, a_s, *, tm, halo, sub):
    i = pl.program_id(1)
    last = pl.num_programs(1) - 1
    x = cur_ref[0]
    xp = jnp.where(i == 0, 0.0, prev_ref[0])
    xn = jnp.where(i == last, 0.0, next_ref[0])
    xb = jnp.concatenate([xp, x, xn], axis=0).astype(bf16)
    rows = tm + 2 * halo

    def conv3(h, w, b):
        below = pltpu.roll(h, 1, axis=0)
        above = pltpu.roll(h, rows - 1, axis=0)
        y = below * w[0:1] + h * w[1:2] + above * w[2:3] + b
        return y[halo:halo + tm]

    for j in range(0, FFN_PAD, sub):
        hg = jnp.dot(xb, wg_ref[:, j:j + sub], preferred_element_type=f32)
        hu = jnp.dot(xb, wu_ref[:, j:j + sub], preferred_element_type=f32)
        yg = conv3(hg, cwg_ref[:, j:j + sub], cbg_ref[:, j:j + sub])
        yu = conv3(hu, cwu_ref[:, j:j + sub], cbu_ref[:, j:j + sub])
        act = 0.5 * yg * (1.0 + lax.erf(yg * (2.0 ** -0.5)))
        a_s[:, j:j + sub] = (act * yu).astype(bf16)
    down = jnp.dot(a_s[...], wd_ref[...], preferred_element_type=f32)
    o_ref[0] = _layer_norm(DN_ALPHA * x + down, g_ref[...], b_ref[...])


def _conv_ffn(x, wg, wu, cwg, cwu, cbg, cbu, wd, ln_g, ln_b, tm):
    bsz, s, _ = x.shape
    halo = 8
    nb = tm // halo
    sub = 2 * LANE
    assert FFN_PAD % sub == 0
    kern = functools.partial(_ffn_kernel, tm=tm, halo=halo, sub=sub)
    full2 = lambda b, i: (0, 0)
    resident = dict(pipeline_mode=pl.Buffered(1))
    return pl.pallas_call(
        kern, grid=(bsz, s // tm),
        in_specs=[pl.BlockSpec((1, halo, D_MODEL), lambda b, i: (b, jnp.maximum(i * nb - 1, 0), 0)),
                  pl.BlockSpec((1, tm, D_MODEL), lambda b, i: (b, i, 0)),
                  pl.BlockSpec((1, halo, D_MODEL),
                               lambda b, i: (b, jnp.minimum((i + 1) * nb, s // halo - 1), 0)),
                  pl.BlockSpec((D_MODEL, FFN_PAD), full2, **resident),
                  pl.BlockSpec((D_MODEL, FFN_PAD), full2, **resident),
                  pl.BlockSpec((FFN_CONV_K, FFN_PAD), full2),
                  pl.BlockSpec((FFN_CONV_K, FFN_PAD), full2),
                  pl.BlockSpec((1, FFN_PAD), full2),
                  pl.BlockSpec((1, FFN_PAD), full2),
                  pl.BlockSpec((FFN_PAD, D_MODEL), full2, **resident),
                  pl.BlockSpec((1, D_MODEL), full2),
                  pl.BlockSpec((1, D_MODEL), full2)],
        out_specs=pl.BlockSpec((1, tm, D_MODEL), lambda b, i: (b, i, 0)),
        out_shape=jax.ShapeDtypeStruct((bsz, s, D_MODEL), f32),
        scratch_shapes=[pltpu.VMEM((tm, FFN_PAD), bf16)],
        compiler_params=_cparams(2), name="conv_ffn")(
            x, x, x, wg, wu, cwg, cwu, cbg, cbu, wd, ln_g, ln_b)


def _t5_bucket(rel):
    half = NUM_BUCKETS // 2
    max_exact = half // 2
    n = jnp.abs(rel)
    nf = jnp.maximum(n, 1).astype(jnp.float32)
    large = max_exact + (jnp.log(nf / max_exact) / math.log(MAX_DISTANCE / max_exact)
                         * (half - max_exact)).astype(jnp.int32)
    large = jnp.minimum(large, half - 1)
    return jnp.where(rel > 0, half, 0) + jnp.where(n < max_exact, n, large)


def _toeplitz(value_of, start, n_keys, n_queries):
    period = n_keys + n_queries
    w = jnp.arange(period)
    vec = value_of(start - jnp.where(w < n_queries, w, w - period))
    flat = jnp.tile(vec, (1,) * (vec.ndim - 1) + (n_keys,))[..., :n_keys * (period - 1)]
    return flat.reshape(vec.shape[:-1] + (n_keys, period - 1))[..., :n_queries]


def _diff_bias_tiles(rel_bias, t):
    table = rel_bias[:, :2 * DIFF_HEADS].reshape(NUM_BUCKETS, DIFF_HEADS, 2).astype(f32) * LOG2E
    tiles = []
    for v in range(-2, 3):
        bias = _toeplitz(lambda rel: jnp.moveaxis(table[_t5_bucket(rel)], 0, -1), v * t, t, t)
        tiles.append(jnp.transpose(bias, (0, 2, 1, 3)).reshape(DIFF_HEADS, t, 2 * t))
    return jnp.stack(tiles, axis=1)


def _window_bias_tiles(rel_bias, tq, span):
    table = rel_bias[:, 2 * DIFF_HEADS:].reshape(NUM_BUCKETS, WIN_KV, WIN_HEADS // WIN_KV).astype(f32) * LOG2E

    def masked_bias(rel):
        vals = jnp.where((jnp.abs(rel) <= WIN)[:, None, None], table[_t5_bucket(rel)], NEG)
        return jnp.moveaxis(vals, 0, -1)

    tiles = []
    for start in (0, -WIN, tq - span):
        bias = _toeplitz(masked_bias, start, span, tq)
        tiles.append(jnp.transpose(bias, (0, 2, 1, 3)).reshape(WIN_KV, span, 4 * tq))
    return jnp.stack(tiles, axis=0)


def _rope_tables(seq):
    rows = seq // GRID_W
    row = jnp.repeat(jnp.arange(rows, dtype=f32), GRID_W)
    col = jnp.tile(jnp.arange(GRID_W, dtype=f32), rows)
    n_freq = AX_DH // 4
    inv = ROPE_THETA ** (-jnp.arange(n_freq, dtype=f32) / n_freq)
    ang_r = row[:, None] * inv
    ang_c = col[:, None] * inv
    cos_h = jnp.concatenate([jnp.cos(ang_r), jnp.cos(ang_r), jnp.cos(ang_c), jnp.cos(ang_c)], -1)
    sin_h = jnp.concatenate([-jnp.sin(ang_r), jnp.sin(ang_r), -jnp.sin(ang_c), jnp.sin(ang_c)], -1)
    return jnp.tile(cos_h, (1, 2)), jnp.tile(sin_h, (1, 2))


def _selectors():
    r = jnp.arange(LANE)[:, None]
    c = jnp.arange(LANE)[None, :]
    eq = jnp.stack([jnp.where((r == c) & (r // DIFF_DH == m), 1.0, 0.0) for m in range(2)])
    c2 = jnp.arange(2 * LANE)[None, :]
    e = jnp.stack([jnp.stack([jnp.where((r // WIN_DH == kv) & (c2 // WIN_DH == g)
                                        & (r % WIN_DH == c2 % WIN_DH), 1.0, 0.0)
                              for g in range(4)]) for kv in range(2)])
    d = jnp.arange(WIN_DH)[:, None]
    ev = jnp.stack([jnp.where(c == kv * WIN_DH + d, 1.0, 0.0) for kv in range(2)])
    return eq.astype(bf16), e.astype(bf16), ev.astype(bf16)


def _encode(x, mem, p, tables):
    bsz, s, _ = x.shape
    t_rows = bsz * s
    eq, e, ev, wb, bw = tables
    ident = jnp.eye(LANE, dtype=bf16)
    ident256 = jnp.eye(2 * LANE, dtype=bf16)
    ident512 = jnp.eye(4 * LANE, dtype=bf16)
    cos_t, sin_t = _rope_tables(s)
    for l in range(DEPTH):
        lam_init = 0.8 - 0.6 * math.exp(-0.3 * l)
        x2 = x.reshape(t_rows, D_MODEL)
        u2 = _project(x2, p['w_in'][l], p['b_in'][l], 1024, 1024, "in_proj")
        u = u2.reshape(bsz, s, N_IN)
        o_a = _conv_branch(u, p['a_conv_w'][l], p['a_conv_b'][l], p['a_ln_g'][l], p['a_ln_b'][l], 512)
        o_b = _diff_attention(u, eq, ident, ident512, wb, p['diff_lam'][l], p['diff_sub_g'][l],
                              lam_init, 512)
        o_c = _window_attention(u, e, ev, ident256, bw, p['win_sink'][l], 256, 512)
        q_rot, k_rot = _axial_rope(u2, cos_t, sin_t, p['ax_qn_g'][l], p['ax_kn_g'][l], s, 1024)
        o_d = _axial_attention(q_rot.reshape(bsz, s, -1), k_rot.reshape(bsz, s, -1), u, e, ev,
                               ident256, 256, 512)
        x2 = _merge(o_a.reshape(t_rows, -1), o_b.reshape(t_rows, -1), o_c.reshape(t_rows, -1),
                    o_d.reshape(t_rows, -1), u2, x2, p['w_branch'][l], p['w_mix_out'][l],
                    p['ln1_g'][l], p['ln1_b'][l], 512)
        kv = _project(mem.reshape(-1, D_MODEL), p['w_xkv'][l], None, 1024, 1024, "mem_proj")
        x = _cross_attention(x2.reshape(bsz, s, D_MODEL), kv.reshape(bsz, -1, 2 * D_MODEL),
                             p['w_xq'][l], p['w_xo'][l], p['ln2_g'][l], p['ln2_b'][l], 512)
        x = _conv_ffn(x, p['wg'][l], p['wu'][l], p['cwg'][l], p['cwu'][l], p['cbg'][l], p['cbu'][l],
                      p['wd'][l], p['ln3_g'][l], p['ln3_b'][l], 512)
    return x


def _pad_last(a, n):
    return jnp.pad(a, [(0, 0)] * (a.ndim - 1) + [(0, n - a.shape[-1])])


def kernel(x_prompt, x_sample, mem_prompt, mem_sample, rel_bias, w_in, b_in, a_conv_w, a_conv_b, a_ln_g,
           a_ln_b, diff_lam, diff_sub_g, win_sink, ax_qn_g, ax_kn_g, w_branch, w_mix_out, ln1_g, ln1_b,
           w_xq, w_xkv, w_xo, ln2_g, ln2_b, w_up, f_conv_w, f_conv_b, w_down, ln3_g, ln3_b):
    depth = w_in.shape[0]
    row = lambda a: a.reshape(depth, 1, a.shape[-1])
    tq_win = 256
    p = {
        'w_in': w_in.astype(bf16), 'b_in': row(b_in),
        'a_conv_w': a_conv_w, 'a_conv_b': row(a_conv_b), 'a_ln_g': row(a_ln_g), 'a_ln_b': row(a_ln_b),
        'diff_lam': diff_lam, 'diff_sub_g': diff_sub_g.reshape(depth, DIFF_VD, 1),
        'win_sink': jnp.repeat(win_sink.reshape(depth, WIN_KV, 1, WIN_HEADS // WIN_KV) * LOG2E, tq_win, axis=-1),
        'ax_qn_g': jnp.tile(row(ax_qn_g), (1, 1, 2)), 'ax_kn_g': jnp.tile(row(ax_kn_g), (1, 1, 2)),
        'w_branch': w_branch.astype(bf16), 'w_mix_out': w_mix_out.astype(bf16),
        'ln1_g': row(ln1_g), 'ln1_b': row(ln1_b),
        'w_xq': w_xq.astype(bf16), 'w_xkv': w_xkv.astype(bf16), 'w_xo': w_xo.astype(bf16),
        'ln2_g': row(ln2_g), 'ln2_b': row(ln2_b),
        'wg': _pad_last(w_up[..., :FFN_DIM], FFN_PAD).astype(bf16),
        'wu': _pad_last(w_up[..., FFN_DIM:], FFN_PAD).astype(bf16),
        'cwg': _pad_last(f_conv_w[..., :FFN_DIM], FFN_PAD), 'cwu': _pad_last(f_conv_w[..., FFN_DIM:], FFN_PAD),
        'cbg': _pad_last(row(f_conv_b[..., :FFN_DIM]), FFN_PAD),
        'cbu': _pad_last(row(f_conv_b[..., FFN_DIM:]), FFN_PAD),
        'wd': jnp.pad(w_down, ((0, 0), (0, FFN_PAD - FFN_DIM), (0, 0))).astype(bf16),
        'ln3_g': row(ln3_g), 'ln3_b': row(ln3_b),
    }
    eq, e, ev = _selectors()
    wb = _diff_bias_tiles(rel_bias, 512)
    bw = _window_bias_tiles(rel_bias, tq_win, 512)
    tables = (eq, e, ev, wb, bw)
    return (_encode(x_prompt, mem_prompt, p, tables), _encode(x_sample, mem_sample, p, tables))
```
